```python
import math
import jax, jax.numpy as jnp
from jax import lax
import numpy as np

D_MODEL = 2048
BATCH = 32
SEQ = 256
DEPTH = 2
DEC_BATCH = 4
DEC_SEQ = 1024
PAST_LEN = 512

GRID_W = 64
D_CONV = D_MODEL // 4
CONV_WIDTH = 31
N_DIFF_HEADS = 8
D_QK = 64
D_V = 2 * D_QK
Q_BLOCK = 128
ROPE_BASE = 10000.0
ROT_HALF = D_QK // 4
HG_HEADS = 4
HG_F = 128
HG_I = (D_MODEL // 4) // HG_HEADS
HG_CHUNK = 32
N_EXPERTS = 32
TOP_K = 4
D_FF = D_MODEL
SWIGLU_ALPHA = 1.702
SWIGLU_LIMIT = 7.0
MOE_BLOCK = 256
DEEPNORM_ALPHA = (2 * DEPTH) ** 0.25
DEEPNORM_BETA = (8 * DEPTH) ** -0.25
LN_EPS = 1e-5
SPLIT_SIZES = (2 * D_CONV, N_DIFF_HEADS * 2 * D_QK, N_DIFF_HEADS * 2 * D_QK, N_DIFF_HEADS * D_V,
               HG_HEADS * HG_F, HG_HEADS * HG_F, HG_HEADS * HG_F, HG_HEADS * HG_I, HG_HEADS * HG_I,
               3 * D_MODEL)
SPLIT_POINTS = tuple(int(s) for s in np.cumsum(SPLIT_SIZES)[:-1])
D_IN = sum(SPLIT_SIZES)

kernel_name = "hybrid_diffusion_prefix_trunk_step"


def layer_norm(x, g=None, b=None):
    xf = x.astype(jnp.float32)
    mu = jnp.mean(xf, axis=-1, keepdims=True)
    var = jnp.mean(jnp.square(xf - mu), axis=-1, keepdims=True)
    y = (xf - mu) * lax.rsqrt(var + LN_EPS)
    if g is not None:
        y = y * g.astype(jnp.float32) + b.astype(jnp.float32)
    return y.astype(x.dtype)


def rms_norm(x, w):
    xf = x.astype(jnp.float32)
    y = xf * lax.rsqrt(jnp.mean(jnp.square(xf), axis=-1, keepdims=True) + LN_EPS)
    return (y * w.astype(jnp.float32)).astype(x.dtype)


def rope_tables(n):
    rows = n // GRID_W
    r, col = jnp.meshgrid(jnp.arange(rows, dtype=jnp.float32), jnp.arange(GRID_W, dtype=jnp.float32), indexing="ij")
    r = r.reshape(-1)
    col = col.reshape(-1)
    freqs = ROPE_BASE ** (-jnp.arange(ROT_HALF, dtype=jnp.float32) / ROT_HALF)
    ar = r[:, None] * freqs
    ac = col[:, None] * freqs
    ang = jnp.concatenate([ar, ar, ac, ac], axis=-1)
    return jnp.cos(ang), jnp.sin(ang)


def apply_rope(x, cos, sin):
    xr = x.reshape(x.shape[:-1] + (2, 2, ROT_HALF))
    rot = jnp.concatenate([-xr[..., 1:, :], xr[..., :1, :]], axis=-2).reshape(x.shape)
    c = cos[None, :, None, None, :].astype(x.dtype)
    s = sin[None, :, None, None, :].astype(x.dtype)
    return x * c + rot * s


def conv_module(glu_in, w_dw, b_dw, g_n, b_n, w_out):
    a, g = jnp.split(glu_in, 2, axis=-1)
    x = a * jax.nn.sigmoid(g)
    x = lax.conv_general_dilated(x, w_dw[:, None, :].astype(x.dtype), window_strides=(1,),
                                 padding=((CONV_WIDTH // 2, CONV_WIDTH // 2),),
                                 dimension_numbers=("NWC", "WIO", "NWC"),
                                 feature_group_count=D_CONV) + b_dw
    x = jax.nn.silu(layer_norm(x, g_n, b_n))
    return x @ w_out


def diff_attention(q, k, v, lam):
    b, sq, h = q.shape[:3]
    nb = sq // Q_BLOCK
    qb = q.reshape(b, nb, Q_BLOCK, h, 2, D_QK).transpose(1, 0, 2, 3, 4, 5)
    vf = v.astype(jnp.float32)
    scale = D_QK ** -0.5

    def block(qi):
        s = jnp.einsum("bqhnd,bkhnd->bhnqk", qi, k).astype(jnp.float32) * scale
        p = jax.nn.softmax(s, axis=-1)
        w = p[:, :, 0] - lam * p[:, :, 1]
        return jnp.einsum("bhqk,bkhv->bqhv", w, vf)

    o = lax.map(block, qb)
    return o.transpose(1, 0, 2, 3, 4).reshape(b, sq, h, D_V)


def hgrn_chunk_scan(q, k, log_f, v, s0):
    b, n, h, _ = q.shape
    nc = n // HG_CHUNK

    def chunks(a):
        return a.astype(jnp.float32).reshape(b, nc, HG_CHUNK, h, a.shape[-1]).transpose(1, 0, 3, 2, 4)

    causal = jnp.tril(jnp.ones((HG_CHUNK, HG_CHUNK), dtype=bool))[:, :, None]

    def step(s, xs):
        qc, kc, lfc, vc = xs
        cum = jnp.cumsum(lfc, axis=2)
        o_inter = jnp.einsum("bhtf,bhfi->bhti", qc * jnp.exp(cum), s)
        rel = cum[:, :, :, None, :] - cum[:, :, None, :, :]
        decay = jnp.exp(jnp.where(causal, rel, -jnp.inf))
        scores = jnp.einsum("bhtf,bhsf,bhtsf->bhts", qc, kc, decay)
        o = o_inter + jnp.einsum("bhts,bhsi->bhti", scores, vc)
        last = cum[:, :, -1:, :]
        s_new = jnp.exp(last[:, :, 0, :, None]) * s + jnp.einsum("bhsf,bhsi->bhfi", kc * jnp.exp(last - cum), vc)
        return s_new, o

    s_fin, o = lax.scan(step, s0.astype(jnp.float32), (chunks(q), chunks(k), chunks(log_f), chunks(v)))
    o = o.transpose(1, 0, 3, 2, 4).reshape(b, n, h, v.shape[-1])
    return o, s_fin


def hgrn_direction(q, f_pre, lb, v, s0, reverse):
    f = lb + (1.0 - lb) * jax.nn.sigmoid(f_pre.astype(jnp.float32))
    log_f = jnp.log(f)
    k = 1.0 - f
    if reverse:
        q, k, log_f, v = q[:, ::-1], k[:, ::-1], log_f[:, ::-1], v[:, ::-1]
    o, s = hgrn_chunk_scan(q, k, log_f, v, s0)
    if reverse:
        o = o[:, ::-1]
    return o, s


def token_mixers(h, lp, layer_idx, rope, ctx):
    b, n, _ = h.shape
    proj = h @ lp["w_in"]
    glu_in, q, k, v, hq, hf_fwd, hf_bwd, hi, hg, gates = jnp.split(proj, SPLIT_POINTS, axis=-1)

    y_a = conv_module(glu_in, lp["conv_w"], lp["conv_b"], lp["conv_norm_g"], lp["conv_norm_b"], lp["w_conv_out"])

    q = q.reshape(b, n, N_DIFF_HEADS, 2, D_QK)
    k = k.reshape(b, n, N_DIFF_HEADS, 2, D_QK)
    v = v.reshape(b, n, N_DIFF_HEADS, D_V)
    if ctx is None:
        k_all, v_all = k, v
        s0 = jnp.zeros((b, 2, HG_HEADS, HG_F, HG_I), jnp.float32)
    else:
        ck, cv, s0 = ctx
        lc = ck.shape[2]
        q = apply_rope(q, *rope)
        k_lat = apply_rope(k, *rope)
        k_all = jnp.concatenate([k_lat, ck.reshape(b, N_DIFF_HEADS, lc, 2, D_QK).transpose(0, 2, 1, 3, 4)], axis=1)
        v_all = jnp.concatenate([v, cv.transpose(0, 2, 1, 3)], axis=1)
    lam_init = 0.8 - 0.6 * math.exp(-0.3 * layer_idx)
    f32 = jnp.float32
    lam = (jnp.exp(jnp.sum(lp["lambda_q1"].astype(f32) * lp["lambda_k1"].astype(f32)))
           - jnp.exp(jnp.sum(lp["lambda_q2"].astype(f32) * lp["lambda_k2"].astype(f32))) + lam_init)
    o_b = diff_attention(q, k_all, v_all, lam)
    o_b = rms_norm(o_b, lp["subln_w"]) * (1.0 - lam_init)
    y_b = o_b.reshape(b, n, N_DIFF_HEADS * D_V).astype(h.dtype) @ lp["w_attn_out"]

    qc = jax.nn.silu(hq).reshape(b, n, HG_HEADS, HG_F)
    ic = hi.reshape(b, n, HG_HEADS, HG_I)
    lb = lp["lb"].reshape(2, HG_HEADS, HG_F)
    o_fw, s_fw = hgrn_direction(qc, hf_fwd.reshape(b, n, HG_HEADS, HG_F), lb[0], ic, s0[:, 0], False)
    o_bw, s_bw = hgrn_direction(qc, hf_bwd.reshape(b, n, HG_HEADS, HG_F), lb[1], ic, s0[:, 1], True)
    o_c = rms_norm((o_fw + o_bw).astype(h.dtype), lp["hgrn_norm_w"]) * jax.nn.silu(hg.reshape(b, n, HG_HEADS, HG_I))
    y_c = o_c.reshape(b, n, HG_HEADS * HG_I) @ lp["w_hgrn_out"]

    g_a, g_b, g_c = jnp.split(jax.nn.sigmoid(gates), 3, axis=-1)
    mix = (g_a * y_a + g_b * y_b + g_c * y_c) @ lp["w_o"]

    if ctx is None:
        new_ctx = (k.transpose(0, 2, 1, 3, 4).reshape(b, N_DIFF_HEADS, n, 2 * D_QK),
                   v.transpose(0, 2, 1, 3),
                   jnp.stack([s_fw, s_bw], axis=1).astype(h.dtype))
    else:
        new_ctx = None
    return mix, new_ctx


def moe_ffn(u, w_router, b_router, w1, b1, w2, b2):
    t = u.reshape(-1, D_MODEL)
    n_tok = t.shape[0]
    logits = (t @ w_router + b_router).astype(jnp.float32)
    top_vals, top_idx = lax.top_k(logits, TOP_K)
    gate_w = jax.nn.softmax(top_vals, axis=-1)
    n_assign = n_tok * TOP_K
    e_flat = top_idx.reshape(-1)
    tok_flat = jnp.arange(n_assign, dtype=jnp.int32) // TOP_K
    g_flat = gate_w.reshape(-1)
    order = jnp.argsort(e_flat)
    e_sorted = e_flat[order]
    counts = jnp.bincount(e_flat, length=N_EXPERTS)
    padded = ((counts + MOE_BLOCK - 1) // MOE_BLOCK) * MOE_BLOCK
    start = jnp.cumsum(counts) - counts
    p_end = jnp.cumsum(padded)
    p_start = p_end - padded
    dest = p_start[e_sorted] + (jnp.arange(n_assign, dtype=jnp.int32) - start[e_sorted])
    n_blocks = -(-n_assign // MOE_BLOCK) + N_EXPERTS
    n_rows = n_blocks * MOE_BLOCK
    row_tok = jnp.zeros((n_rows,), jnp.int32).at[dest].set(tok_flat[order])
    row_gate = jnp.zeros((n_rows,), jnp.float32).at[dest].set(g_flat[order])
    blk_expert = jnp.minimum(jnp.searchsorted(p_end, jnp.arange(n_blocks, dtype=jnp.int32) * MOE_BLOCK, side="right"),
                             N_EXPERTS - 1)

    def expert_block(args):
        tok, gate, e = args
        hdn = t[tok] @ w1[e] + b1[e]
        glu = jnp.minimum(hdn[:, 0::2], SWIGLU_LIMIT)
        lin = jnp.clip(hdn[:, 1::2], -SWIGLU_LIMIT, SWIGLU_LIMIT)
        act = glu * jax.nn.sigmoid(SWIGLU_ALPHA * glu) * (lin + 1.0)
        return (act @ w2[e] + b2[e]) * gate[:, None]

    out = lax.map(expert_block, (row_tok.reshape(n_blocks, MOE_BLOCK), row_gate.reshape(n_blocks, MOE_BLOCK), blk_expert))
    y = jnp.zeros((n_tok, D_MODEL), jnp.float32).at[row_tok].add(out.reshape(n_rows, D_MODEL).astype(jnp.float32))
    return y.astype(u.dtype).reshape(u.shape)


def trunk_layer(x, cond, lp, layer_idx, rope, ctx):
    mod = jax.nn.silu(cond) @ lp["w_mod"] + lp["b_mod"]
    sh1, sc1, g1, sh2, sc2, g2 = [m[:, None, :] for m in jnp.split(mod, 6, axis=-1)]
    u = layer_norm(x) * (1.0 + sc1) + sh1
    mix, new_ctx = token_mixers(u, lp, layer_idx, rope, ctx)
    x = layer_norm(DEEPNORM_ALPHA * x + g1 * mix, lp["ln1_g"], lp["ln1_b"])
    u = layer_norm(x) * (1.0 + sc2) + sh2
    ffn = moe_ffn(u, lp["w_router"], lp["b_router"], lp["w1"], lp["b1"], lp["w2"], lp["b2"])
    x = layer_norm(DEEPNORM_ALPHA * x + g2 * ffn, lp["ln2_g"], lp["ln2_b"])
    return x, new_ctx


def setup_inputs(seed: int = 0) -> dict:
    key = jax.random.key(seed)
    ks = jax.random.split(key, 40)
    f32 = jnp.float32

    def nrm(k, shape, scale):
        return scale * jax.random.normal(k, shape, f32)

    s_d = D_MODEL ** -0.5
    beta = DEEPNORM_BETA
    return {
        "x_prompt": nrm(ks[0], (BATCH, SEQ, D_MODEL), 1.0),
        "x_sample": nrm(ks[1], (DEC_BATCH, DEC_SEQ, D_MODEL), 1.0),
        "c": nrm(ks[2], (DEC_BATCH, D_MODEL), 1.0),
        "cache_k": nrm(ks[3], (DEC_BATCH, DEPTH, N_DIFF_HEADS, PAST_LEN, 2 * D_QK), 1.0),
        "cache_v": nrm(ks[4], (DEC_BATCH, DEPTH, N_DIFF_HEADS, PAST_LEN, D_V), 1.0),
        "state_hgrn": nrm(ks[5], (DEC_BATCH, DEPTH, 2, HG_HEADS, HG_F, HG_I), 0.5),
        "c_ctx": nrm(ks[6], (D_MODEL,), 1.0),
        "w_mod": nrm(ks[7], (DEPTH, D_MODEL, 6 * D_MODEL), 0.5 * s_d),
        "b_mod": nrm(ks[8], (DEPTH, 6 * D_MODEL), 0.02),
        "w_in": nrm(ks[9], (DEPTH, D_MODEL, D_IN), s_d),
        "conv_w": nrm(ks[10], (DEPTH, CONV_WIDTH, D_CONV), CONV_WIDTH ** -0.5),
        "conv_b": nrm(ks[11], (DEPTH, D_CONV), 0.02),
        "conv_norm_g": 1.0 + nrm(ks[12], (DEPTH, D_CONV), 0.02),
        "conv_norm_b": nrm(ks[13], (DEPTH, D_CONV), 0.02),
        "w_conv_out": nrm(ks[14], (DEPTH, D_CONV, D_MODEL), beta * D_CONV ** -0.5),
        "lambda_q1": nrm(ks[15], (DEPTH, D_QK), 0.1),
        "lambda_k1": nrm(ks[16], (DEPTH, D_QK), 0.1),
        "lambda_q2": nrm(ks[17], (DEPTH, D_QK), 0.1),
        "lambda_k2": nrm(ks[18], (DEPTH, D_QK), 0.1),
        "subln_w": 1.0 + nrm(ks[19], (DEPTH, D_V), 0.02),
        "w_attn_out": nrm(ks[20], (DEPTH, N_DIFF_HEADS * D_V, D_MODEL), beta * (N_DIFF_HEADS * D_V) ** -0.5),
        "lb_param": nrm(ks[21], (DEPTH, 2, HG_HEADS * HG_F), 1.0),
        "hgrn_norm_w": 1.0 + nrm(ks[22], (DEPTH, HG_I), 0.02),
        "w_hgrn_out": nrm(ks[23], (DEPTH, HG_HEADS * HG_I, D_MODEL), beta * (HG_HEADS * HG_I) ** -0.5),
        "w_o": nrm(ks[24], (DEPTH, D_MODEL, D_MODEL), beta * s_d),
        "ln1_g": 1.0 + nrm(ks[25], (DEPTH, D_MODEL), 0.02),
        "ln1_b": nrm(ks[26], (DEPTH, D_MODEL), 0.02),
        "ln2_g": 1.0 + nrm(ks[27], (DEPTH, D_MODEL), 0.02),
        "ln2_b": nrm(ks[28], (DEPTH, D_MODEL), 0.02),
        "w_router": nrm(ks[29], (DEPTH, D_MODEL, N_EXPERTS), s_d),
        "b_router": nrm(ks[30], (DEPTH, N_EXPERTS), 0.01),
        "w1": nrm(ks[31], (DEPTH, N_EXPERTS, D_MODEL, 2 * D_FF), s_d),
        "b1": nrm(ks[32], (DEPTH, N_EXPERTS, 2 * D_FF), 0.02),
        "w2": nrm(ks[33], (DEPTH, N_EXPERTS, D_FF, D_MODEL), beta * D_FF ** -0.5),
        "b2": nrm(ks[34], (DEPTH, N_EXPERTS, D_MODEL), 0.02),
    }


def reference(x_prompt, x_sample, c, cache_k, cache_v, state_hgrn, c_ctx,
              w_mod, b_mod, w_in, conv_w, conv_b, conv_norm_g, conv_norm_b, w_conv_out,
              lambda_q1, lambda_k1, lambda_q2, lambda_k2, subln_w, w_attn_out,
              lb_param, hgrn_norm_w, w_hgrn_out, w_o, ln1_g, ln1_b, ln2_g, ln2_b,
              w_router, b_router, w1, b1, w2, b2):
    lb_all = jnp.cumsum(jax.nn.softmax(lb_param.astype(jnp.float32), axis=0), axis=0)
    lb_all = lb_all - lb_all[:1]

    layers = []
    for l in range(DEPTH):
        layers.append({
            "w_mod": w_mod[l], "b_mod": b_mod[l], "w_in": w_in[l],
            "conv_w": conv_w[l], "conv_b": conv_b[l], "conv_norm_g": conv_norm_g[l], "conv_norm_b": conv_norm_b[l],
            "w_conv_out": w_conv_out[l],
            "lambda_q1": lambda_q1[l], "lambda_k1": lambda_k1[l], "lambda_q2": lambda_q2[l], "lambda_k2": lambda_k2[l],
            "subln_w": subln_w[l], "w_attn_out": w_attn_out[l],
            "lb": lb_all[l], "hgrn_norm_w": hgrn_norm_w[l], "w_hgrn_out": w_hgrn_out[l],
            "w_o": w_o[l], "ln1_g": ln1_g[l], "ln1_b": ln1_b[l], "ln2_g": ln2_g[l], "ln2_b": ln2_b[l],
            "w_router": w_router[l], "b_router": b_router[l],
            "w1": w1[l], "b1": b1[l], "w2": w2[l], "b2": b2[l],
        })

    x = x_prompt
    cond_ctx = c_ctx[None, :]
    ks_new, vs_new, ss_new = [], [], []
    for l in range(DEPTH):
        x, (k_l, v_l, s_l) = trunk_layer(x, cond_ctx, layers[l], l, None, None)
        ks_new.append(k_l)
        vs_new.append(v_l)
        ss_new.append(s_l)
    y_prompt = x
    new_cache_k = jnp.stack(ks_new, axis=1)
    new_cache_v = jnp.stack(vs_new, axis=1)
    new_state_hgrn = jnp.stack(ss_new, axis=1)

    rope = rope_tables(x_sample.shape[1])
    x = x_sample
    for l in range(DEPTH):
        x, _ = trunk_layer(x, c, layers[l], l, rope, (cache_k[:, l], cache_v[:, l], state_hgrn[:, l]))
    y_sample = x

    return (y_prompt, y_sample, new_cache_k, new_cache_v, new_state_hgrn)
```

```python
import functools
import math

import jax
import jax.numpy as jnp
from jax import lax
from jax.experimental import pallas as pl
from jax.experimental.pallas import tpu as pltpu

F32 = jnp.float32
BF16 = jnp.bfloat16

D_MODEL = 2048
BATCH = 32
SEQ = 256
DEPTH = 2
DEC_BATCH = 4
DEC_SEQ = 1024
PAST_LEN = 512
GRID_W = 64
D_CONV = D_MODEL // 4
CONV_WIDTH = 31
N_DIFF_HEADS = 8
D_QK = 64
D_V = 2 * D_QK
ROPE_BASE = 10000.0
ROT_HALF = D_QK // 4
HG_HEADS = 4
HG_F = 128
HG_I = (D_MODEL // 4) // HG_HEADS
N_EXPERTS = 32
TOP_K = 4
D_FF = D_MODEL
SWIGLU_ALPHA = 1.702
SWIGLU_LIMIT = 7.0
DEEPNORM_ALPHA = (2 * DEPTH) ** 0.25
LN_EPS = 1e-5

T_CTX = BATCH * SEQ
T_LAT = DEC_BATCH * DEC_SEQ
T_ALL = T_CTX + T_LAT
N_MOD_ROWS = 8

COL_Q = (2 * D_CONV) // 128
COL_K = COL_Q + N_DIFF_HEADS
COL_V = COL_K + N_DIFF_HEADS
COL_HQ = COL_V + N_DIFF_HEADS
COL_HF_FWD = COL_HQ + HG_HEADS
COL_HF_BWD = COL_HF_FWD + HG_HEADS
COL_HI = COL_HF_BWD + HG_HEADS
COL_HG = COL_HI + HG_HEADS
COL_GATES = COL_HG + HG_HEADS
D_IN = COL_GATES * 128 + 3 * D_MODEL

HG_CHUNK = 16
ATTN_TQ = 256
CONV_ROWS = 64
CONV_PAD = 16
MOE_TM = 1024
MOE_SUB = 256
MOE_TN = 512
MOE_BLOCKS = (T_ALL * TOP_K) // MOE_TM + N_EXPERTS
MOE_ROWS = MOE_BLOCKS * MOE_TM

MIB = 1024 * 1024


def _params(sem, vmem_mib):
    return pltpu.CompilerParams(dimension_semantics=sem, vmem_limit_bytes=vmem_mib * MIB)


def _mod_row(i, tm):
    r0 = i * tm
    return jnp.where(r0 < T_CTX, 0, 1 + (r0 - T_CTX) // DEC_SEQ)


def _ln(x):
    mu = jnp.mean(x, axis=-1, keepdims=True)
    xc = x - mu
    var = jnp.mean(xc * xc, axis=-1, keepdims=True)
    return xc * lax.rsqrt(var + LN_EPS)


def _bdot(a, b):
    return jnp.dot(a.astype(BF16), b.astype(BF16), preferred_element_type=F32)


def _mod_kernel(c_ref, w_ref, b_ref, o_ref):
    c = c_ref[...]
    o_ref[...] = _bdot(c * jax.nn.sigmoid(c), w_ref[...]) + b_ref[...]


def _modulation(cond, w_mod, b_mod3, l):
    tn = 1024
    return pl.pallas_call(
        _mod_kernel,
        grid=(6 * D_MODEL // tn,),
        in_specs=[
            pl.BlockSpec((N_MOD_ROWS, D_MODEL), lambda j: (0, 0)),
            pl.BlockSpec((None, D_MODEL, tn), lambda j: (l, 0, j)),
            pl.BlockSpec((None, 1, tn), lambda j: (l, 0, j)),
        ],
        out_specs=pl.BlockSpec((N_MOD_ROWS, tn), lambda j: (0, j)),
        out_shape=jax.ShapeDtypeStruct((N_MOD_ROWS, 6 * D_MODEL), F32),
        compiler_params=_params(("arbitrary",), 40),
        name="modulation",
    )(cond, w_mod, b_mod3)


def _in_proj_kernel(x_ref, sh_ref, sc_ref, w_ref, o_ref, u_ref):
    @pl.when(pl.program_id(1) == 0)
    def _():
        u_ref[...] = (_ln(x_ref[...]) * (1.0 + sc_ref[0]) + sh_ref[0]).astype(BF16)

    o_ref[...] = jnp.dot(u_ref[...], w_ref[...].astype(BF16), preferred_element_type=F32)


def _in_proj(x, mod3, w_in, l):
    tm, tn = 1024, 512
    row = functools.partial(_mod_row, tm=tm)
    return pl.pallas_call(
        _in_proj_kernel,
        grid=(T_ALL // tm, D_IN // tn),
        in_specs=[
            pl.BlockSpec((tm, D_MODEL), lambda i, j: (i, 0)),
            pl.BlockSpec((1, 1, D_MODEL), lambda i, j: (row(i), 0, 0)),
            pl.BlockSpec((1, 1, D_MODEL), lambda i, j: (row(i), 0, 1)),
            pl.BlockSpec((None, D_MODEL, tn), lambda i, j: (l, 0, j)),
        ],
        out_specs=pl.BlockSpec((tm, tn), lambda i, j: (i, j)),
        out_shape=jax.ShapeDtypeStruct((T_ALL, D_IN), F32),
        scratch_shapes=[pltpu.VMEM((tm, D_MODEL), BF16)],
        compiler_params=_params(("parallel", "arbitrary"), 48),
        name="in_proj",
    )(x, mod3, mod3, w_in)


def _conv_kernel(glu_ref, w_ref, b_ref, g_ref, beta_ref, o_ref, pad_ref, *, n):
    a = glu_ref[:, :D_CONV]
    g = glu_ref[:, D_CONV:]
    zeros = jnp.zeros((CONV_PAD, D_CONV), F32)
    pad_ref[0:CONV_PAD, :] = zeros
    pad_ref[CONV_PAD + n:CONV_PAD + n + CONV_PAD, :] = zeros
    pad_ref[CONV_PAD:CONV_PAD + n, :] = a * jax.nn.sigmoid(g)
    first = CONV_PAD - CONV_WIDTH // 2
    for r0 in range(0, n, CONV_ROWS):
        acc = jnp.zeros((CONV_ROWS, D_CONV), F32)
        for j in range(CONV_WIDTH):
            s = r0 + first + j
            acc = acc + w_ref[j:j + 1, :] * pad_ref[s:s + CONV_ROWS, :]
        y = _ln(acc + b_ref[...]) * g_ref[...] + beta_ref[...]
        o_ref[r0:r0 + CONV_ROWS, :] = (y * jax.nn.sigmoid(y)).astype(BF16)


def _conv_branch(proj, conv_w, conv_b3, conv_g3, conv_beta3, l, nb, n, row_off):
    off = row_off // n
    vec = pl.BlockSpec((None, 1, D_CONV), lambda b: (l, 0, 0))
    return pl.pallas_call(
        functools.partial(_conv_kernel, n=n),
        grid=(nb,),
        in_specs=[
            pl.BlockSpec((n, 2 * D_CONV), lambda b: (off + b, 0)),
            pl.BlockSpec((None, CONV_WIDTH, D_CONV), lambda b: (l, 0, 0)),
            vec, vec, vec,
        ],
        out_specs=pl.BlockSpec((n, D_CONV), lambda b: (b, 0)),
        out_shape=jax.ShapeDtypeStruct((nb * n, D_CONV), BF16),
        scratch_shapes=[pltpu.VMEM((n + 2 * CONV_PAD, D_CONV), F32)],
        compiler_params=_params(("parallel",), 40),
        name="conv_branch",
    )(proj, conv_w, conv_b3, conv_g3, conv_beta3)


def _rope(x, cos, sin):
    lane = lax.broadcasted_iota(jnp.int32, x.shape, 1)
    nxt = pltpu.roll(x, 128 - ROT_HALF, 1)
    prv = pltpu.roll(x, ROT_HALF, 1)
    rot = jnp.where((lane % (2 * ROT_HALF)) < ROT_HALF, -nxt, prv)
    return x * cos + rot * sin


def _attn_kernel(*refs, n, latent, lam_init):
    if latent:
        lam_ref, q_ref, k_ref, v_ref, ck_ref, cv_ref, cos_ref, sin_ref, sw_ref, o_ref = refs
    else:
        lam_ref, q_ref, k_ref, v_ref, sw_ref, o_ref, ko_ref, vo_ref = refs
    lam = lam_ref[0, 0]
    scale = D_QK ** -0.5
    k = k_ref[...]
    v = v_ref[...]
    if latent:
        k = _rope(k, cos_ref[...], sin_ref[...])
        ckb = ck_ref[...].astype(BF16)
        cvb = cv_ref[...].astype(BF16)
    else:
        ko_ref[...] = k
        vo_ref[...] = v
    kb = k.astype(BF16)
    vb = v.astype(BF16)
    nt = (((1,), (1,)), ((), ()))
    first_sub = lax.broadcasted_iota(jnp.int32, (1, 2 * D_QK), 1) < D_QK

    def softmax_maps(qx):
        s = lax.dot_general(qx, kb, nt, preferred_element_type=F32) * scale
        m = jnp.max(s, axis=-1, keepdims=True)
        if latent:
            sc = lax.dot_general(qx, ckb, nt, preferred_element_type=F32) * scale
            m = jnp.maximum(m, jnp.max(sc, axis=-1, keepdims=True))
            e = jnp.exp(s - m)
            ec = jnp.exp(sc - m)
            den = jnp.sum(e, axis=-1, keepdims=True) + jnp.sum(ec, axis=-1, keepdims=True)
            return e / den, ec / den
        e = jnp.exp(s - m)
        return e / jnp.sum(e, axis=-1, keepdims=True), None

    for r0 in range(0, n, ATTN_TQ):
        q = q_ref[r0:r0 + ATTN_TQ, :]
        if latent:
            q = _rope(q, cos_ref[r0:r0 + ATTN_TQ, :], sin_ref[r0:r0 + ATTN_TQ, :])
        p1, pc1 = softmax_maps(jnp.where(first_sub, q, 0.0).astype(BF16))
        p2, pc2 = softmax_maps(jnp.where(first_sub, 0.0, q).astype(BF16))
        o = _bdot(p1 - lam * p2, vb)
        if latent:
            o = o + _bdot(pc1 - lam * pc2, cvb)
        o = o * lax.rsqrt(jnp.mean(o * o, axis=-1, keepdims=True) + LN_EPS) * sw_ref[...]
        o_ref[r0:r0 + ATTN_TQ, :] = (o * (1.0 - lam_init)).astype(BF16)


def _attn_branch(proj, lam, subln3, l, nb, n, row_off, ctx=None):
    off = row_off // n
    latent = ctx is not None
    lam_init = 0.8 - 0.6 * math.exp(-0.3 * l)
    tok = lambda col: pl.BlockSpec((n, 128), lambda b, h: (off + b, col + h))
    in_specs = [pl.BlockSpec(memory_space=pltpu.SMEM), tok(COL_Q), tok(COL_K), tok(COL_V)]
    args = [lam, proj, proj, proj]
    o_spec = pl.BlockSpec((n, D_V), lambda b, h: (b, h))
    o_shape = jax.ShapeDtypeStruct((nb * n, N_DIFF_HEADS * D_V), BF16)
    if latent:
        cache_k, cache_v, cos, sin = ctx
        cache = pl.BlockSpec((None, None, None, PAST_LEN, 128), lambda b, h: (b, l, h, 0, 0))
        table = pl.BlockSpec((n, 128), lambda b, h: (0, 0))
        in_specs += [cache, cache, table, table]
        args += [cache_k, cache_v, cos, sin]
        out_specs, out_shape = o_spec, o_shape
    else:
        kv_spec = pl.BlockSpec((None, None, n, 128), lambda b, h: (b, h, 0, 0))
        kv_shape = jax.ShapeDtypeStruct((nb, N_DIFF_HEADS, n, 128), F32)
        out_specs, out_shape = [o_spec, kv_spec, kv_spec], [o_shape, kv_shape, kv_shape]
    in_specs.append(pl.BlockSpec((None, 1, D_V), lambda b, h: (l, 0, 0)))
    args.append(subln3)
    return pl.pallas_call(
        functools.partial(_attn_kernel, n=n, latent=latent, lam_init=lam_init),
        grid=(nb, N_DIFF_HEADS),
        in_specs=in_specs,
        out_specs=out_specs,
        out_shape=out_shape,
        compiler_params=_params(("parallel", "parallel"), 40),
        name="attn_latent" if latent else "attn_context",
    )(*args)


def _hgrn_kernel(*refs, n, has_state):
    if has_state:
        (hq_ref, ff_ref, fb_ref, hi_ref, hg_ref, lb_ref, nw_ref, s0_ref, o_ref,
         q_scr, lf_scr, k_scr, o_scr, st_scr) = refs
    else:
        (hq_ref, ff_ref, fb_ref, hi_ref, hg_ref, lb_ref, nw_ref, o_ref, so_ref,
         q_scr, lf_scr, k_scr, o_scr, st_scr) = refs
    C = HG_CHUNK
    nc = n // C
    hq = hq_ref[...]
    q_scr[...] = hq * jax.nn.sigmoid(hq)
    for d, f_ref in enumerate((ff_ref, fb_ref)):
        lb = lb_ref[d:d + 1, :]
        f = lb + (1.0 - lb) * jax.nn.sigmoid(f_ref[...])
        lf_scr[d] = jnp.log(f)
        k_scr[d] = 1.0 - f
        st_scr[d] = s0_ref[d].T if has_state else jnp.zeros((HG_I, HG_F), F32)

    row = lax.broadcasted_iota(jnp.int32, (C, C), 0)
    col = lax.broadcasted_iota(jnp.int32, (C, C), 1)
    tri = ((col <= row).astype(F32), (col >= row).astype(F32))
    srow = lax.broadcasted_iota(jnp.int32, (C, 1), 0)
    nt = (((1,), (1,)), ((), ()))
    tn = (((0,), (0,)), ((), ()))

    def chunk(d, r0):
        qc = q_scr[pl.ds(r0, C), :]
        kc = k_scr[d, pl.ds(r0, C), :]
        vc = hi_ref[pl.ds(r0, C), :]
        cum = jnp.dot(tri[d], lf_scr[d, pl.ds(r0, C), :],
                      precision=lax.Precision.HIGHEST, preferred_element_type=F32)
        edge = cum[C - 1:C, :] if d == 0 else cum[0:1, :]
        st = st_scr[d]
        o = lax.dot_general((qc * jnp.exp(cum)).astype(BF16), st.astype(BF16), nt,
                            preferred_element_type=F32)
        scores_t = jnp.zeros((C, C), F32)
        for t in range(C):
            seen = (srow <= t) if d == 0 else (srow >= t)
            rel = jnp.where(seen, cum[t:t + 1, :] - cum, -jnp.inf)
            part = jnp.sum(jnp.exp(rel) * kc * qc[t:t + 1, :], axis=-1, keepdims=True)
            scores_t = jnp.where(col == t, part, scores_t)
        o = o + lax.dot_general(scores_t.astype(BF16), vc.astype(BF16), tn,
                                preferred_element_type=F32)
        o_scr[d, pl.ds(r0, C), :] = o
        kd = kc * jnp.exp(edge - cum)
        st_scr[d] = jnp.exp(edge) * st + lax.dot_general(
            vc.astype(BF16), kd.astype(BF16), tn, preferred_element_type=F32)

    def body(c, carry):
        chunk(0, pl.multiple_of(c * C, C))
        chunk(1, pl.multiple_of((nc - 1 - c) * C, C))
        return carry

    lax.fori_loop(0, nc, body, 0)

    o = o_scr[0] + o_scr[1]
    o = o * lax.rsqrt(jnp.mean(o * o, axis=-1, keepdims=True) + LN_EPS) * nw_ref[...]
    hg = hg_ref[...]
    o_ref[...] = (o * (hg * jax.nn.sigmoid(hg))).astype(BF16)
    if not has_state:
        so_ref[0] = st_scr[0].T
        so_ref[1] = st_scr[1].T


def _hgrn_branch(proj, lb_all, nw3, l, nb, n, row_off, state=None):
    off = row_off // n
    has_state = state is not None
    tok = lambda col: pl.BlockSpec((n, 128), lambda b, h: (off + b, col + h))
    in_specs = [tok(COL_HQ), tok(COL_HF_FWD), tok(COL_HF_BWD), tok(COL_HI), tok(COL_HG),
                pl.BlockSpec((None, 2, HG_F), lambda b, h: (l, 0, h)),
                pl.BlockSpec((None, 1, HG_I), lambda b, h: (l, 0, 0))]
    args = [proj] * 5 + [lb_all, nw3]
    o_spec = pl.BlockSpec((n, HG_I), lambda b, h: (b, h))
    o_shape = jax.ShapeDtypeStruct((nb * n, HG_HEADS * HG_I), BF16)
    if has_state:
        in_specs.append(pl.BlockSpec((None, None, 2, None, HG_F, HG_I), lambda b, h: (b, l, 0, h, 0, 0)))
        args.append(state)
        out_specs, out_shape = o_spec, o_shape
    else:
        out_specs = [o_spec, pl.BlockSpec((None, 2, None, HG_F, HG_I), lambda b, h: (b, 0, h, 0, 0))]
        out_shape = [o_shape, jax.ShapeDtypeStruct((nb, 2, HG_HEADS, HG_F, HG_I), F32)]
    return pl.pallas_call(
        functools.partial(_hgrn_kernel, n=n, has_state=has_state),
        grid=(nb, HG_HEADS),
        in_specs=in_specs,
        out_specs=out_specs,
        out_shape=out_shape,
        scratch_shapes=[pltpu.VMEM((n, HG_F), F32), pltpu.VMEM((2, n, HG_F), F32),
                        pltpu.VMEM((2, n, HG_F), F32), pltpu.VMEM((2, n, HG_I), F32),
                        pltpu.VMEM((2, HG_I, HG_F), F32)],
        compiler_params=_params(("parallel", "parallel"), 40),
        name="hgrn_latent" if has_state else "hgrn_context",
    )(*args)


def _mix_kernel(ca_ref, ob_ref, oc_ref, ga_ref, gb_ref, gc_ref, wc_ref, wa_ref, wh_ref, wo_ref,
                x_ref, g1_ref, sh2_ref, sc2_ref, lng_ref, lnb_ref, wr_ref, br_ref,
                x1_ref, u2_ref, lg_ref, acc_ref):
    j = pl.program_id(1)
    merged = (jax.nn.sigmoid(ga_ref[...]) * _bdot(ca_ref[...], wc_ref[...])
              + jax.nn.sigmoid(gb_ref[...]) * _bdot(ob_ref[...], wa_ref[...])
              + jax.nn.sigmoid(gc_ref[...]) * _bdot(oc_ref[...], wh_ref[...]))
    part = _bdot(merged, wo_ref[...])

    @pl.when(j == 0)
    def _():
        acc_ref[...] = part

    @pl.when(j > 0)
    def _():
        acc_ref[...] += part

    @pl.when(j == pl.num_programs(1) - 1)
    def _():
        x1 = _ln(DEEPNORM_ALPHA * x_ref[...] + g1_ref[0] * acc_ref[...]) * lng_ref[...] + lnb_ref[...]
        x1_ref[...] = x1
        u2 = _ln(x1) * (1.0 + sc2_ref[0]) + sh2_ref[0]
        u2_ref[...] = u2.astype(BF16)
        lg_ref[...] = jnp.dot(u2, wr_ref[...], precision=lax.Precision.HIGHEST,
                              preferred_element_type=F32) + br_ref[...]


def _mix(ca, ob, oc, proj, x, mod3, w_conv_out, w_attn_out, w_hgrn_out, w_o, ln_g3, ln_b3,
         w_router, b_router3, l):
    tm, tn = 512, 256
    nj = D_MODEL // tn
    gate0 = COL_GATES * 128 // tn
    row = functools.partial(_mod_row, tm=tm)
    mod_spec = lambda k: pl.BlockSpec((1, 1, D_MODEL), lambda i, j: (row(i), 0, k))
    vec = pl.BlockSpec((None, 1, D_MODEL), lambda i, j: (l, 0, 0))
    return pl.pallas_call(
        _mix_kernel,
        grid=(T_ALL // tm, nj),
        in_specs=[
            pl.BlockSpec((tm, D_CONV), lambda i, j: (i, 0)),
            pl.BlockSpec((tm, N_DIFF_HEADS * D_V), lambda i, j: (i, 0)),
            pl.BlockSpec((tm, HG_HEADS * HG_I), lambda i, j: (i, 0)),
            pl.BlockSpec((tm, tn), lambda i, j: (i, gate0 + j)),
            pl.BlockSpec((tm, tn), lambda i, j: (i, gate0 + nj + j)),
            pl.BlockSpec((tm, tn), lambda i, j: (i, gate0 + 2 * nj + j)),
            pl.BlockSpec((None, D_CONV, tn), lambda i, j: (l, 0, j)),
            pl.BlockSpec((None, N_DIFF_HEADS * D_V, tn), lambda i, j: (l, 0, j)),
            pl.BlockSpec((None, HG_HEADS * HG_I, tn), lambda i, j: (l, 0, j)),
            pl.BlockSpec((None, tn, D_MODEL), lambda i, j: (l, j, 0)),
            pl.BlockSpec((tm, D_MODEL), lambda i, j: (i, 0)),
            mod_spec(2), mod_spec(3), mod_spec(4),
            vec, vec,
            pl.BlockSpec((None, D_MODEL, N_EXPERTS), lambda i, j: (l, 0, 0)),
            pl.BlockSpec((None, 1, N_EXPERTS), lambda i, j: (l, 0, 0)),
        ],
        out_specs=[
            pl.BlockSpec((tm, D_MODEL), lambda i, j: (i, 0)),
            pl.BlockSpec((tm, D_MODEL), lambda i, j: (i, 0)),
            pl.BlockSpec((tm, N_EXPERTS), lambda i, j: (i, 0)),
        ],
        out_shape=[
            jax.ShapeDtypeStruct((T_ALL, D_MODEL), F32),
            jax.ShapeDtypeStruct((T_ALL, D_MODEL), BF16),
            jax.ShapeDtypeStruct((T_ALL, N_EXPERTS), F32),
        ],
        scratch_shapes=[pltpu.VMEM((tm, D_MODEL), F32)],
        compiler_params=_params(("parallel", "arbitrary"), 56),
        name="mix",
    )(ca, ob, oc, proj, proj, proj, w_conv_out, w_attn_out, w_hgrn_out, w_o, x, mod3, mod3, mod3,
      ln_g3, ln_b3, w_router, b_router3)


def _moe_kernel(be_ref, nv_ref, x_ref, w1_ref, b1_ref, w2_ref, b2_ref, g_ref, o_ref, w1b_ref, w2b_ref):
    blk = pl.program_id(0)
    j = pl.program_id(1)
    nvalid = nv_ref[blk]

    @pl.when(j == 0)
    def _():
        o_ref[...] = jnp.zeros_like(o_ref)

    @pl.when(nvalid > 0)
    def _():
        w1b_ref[...] = w1_ref[...].astype(BF16)
        w2b_ref[...] = w2_ref[...].astype(BF16)
        r = lax.broadcasted_iota(jnp.int32, (MOE_TN, MOE_TN // 2), 0)
        c = lax.broadcasted_iota(jnp.int32, (MOE_TN, MOE_TN // 2), 1)
        pick_even = (r == 2 * c).astype(BF16)

        def body(s, carry):
            r0 = pl.multiple_of(s * MOE_SUB, MOE_SUB)
            h = jnp.dot(x_ref[pl.ds(r0, MOE_SUB), :], w1b_ref[...],
                        preferred_element_type=F32) + b1_ref[...]
            glu = jnp.minimum(h, SWIGLU_LIMIT)
            lin = jnp.clip(pltpu.roll(h, MOE_TN - 1, 1), -SWIGLU_LIMIT, SWIGLU_LIMIT)
            act = glu * jax.nn.sigmoid(SWIGLU_ALPHA * glu) * (lin + 1.0)
            act = jnp.dot(act.astype(BF16), pick_even, preferred_element_type=F32)
            o_ref[pl.ds(r0, MOE_SUB), :] += jnp.dot(act.astype(BF16), w2b_ref[...],
                                                    preferred_element_type=F32)
            return carry

        lax.fori_loop(0, (nvalid + MOE_SUB - 1) // MOE_SUB, body, 0)

    @pl.when(j == pl.num_programs(1) - 1)
    def _():
        o_ref[...] = (o_ref[...] + b2_ref[...]) * g_ref[...]


def _moe(x_rows, row_gate, blk_expert, blk_valid, w1, b1_4, w2, b2_4, l):
    nj = 2 * D_FF // MOE_TN
    jj = lambda b, j, nv: jnp.where(nv[b] > 0, j, nj - 1)
    grid_spec = pltpu.PrefetchScalarGridSpec(
        num_scalar_prefetch=2,
        grid=(MOE_BLOCKS, nj),
        in_specs=[
            pl.BlockSpec((MOE_TM, D_MODEL), lambda b, j, be, nv: (b, 0)),
            pl.BlockSpec((None, None, D_MODEL, MOE_TN), lambda b, j, be, nv: (l, be[b], 0, jj(b, j, nv))),
            pl.BlockSpec((None, None, 1, MOE_TN), lambda b, j, be, nv: (l, be[b], 0, jj(b, j, nv))),
            pl.BlockSpec((None, None, MOE_TN // 2, D_MODEL), lambda b, j, be, nv: (l, be[b], jj(b, j, nv), 0)),
            pl.BlockSpec((None, None, 1, D_MODEL), lambda b, j, be, nv: (l, be[b], 0, 0)),
            pl.BlockSpec((MOE_TM, 1), lambda b, j, be, nv: (b, 0)),
        ],
        out_specs=pl.BlockSpec((MOE_TM, D_MODEL), lambda b, j, be, nv: (b, 0)),
        scratch_shapes=[pltpu.VMEM((D_MODEL, MOE_TN), BF16), pltpu.VMEM((MOE_TN // 2, D_MODEL), BF16)],
    )
    return pl.pallas_call(
        _moe_kernel,
        grid_spec=grid_spec,
        out_shape=jax.ShapeDtypeStruct((MOE_ROWS, D_MODEL), F32),
        compiler_params=_params(("arbitrary", "arbitrary"), 56),
        name="moe",
    )(blk_expert, blk_valid, x_rows, w1, b1_4, w2, b2_4, row_gate)


def _route(logits):
    n_assign = T_ALL * TOP_K
    top_vals, top_idx = lax.top_k(logits, TOP_K)
    gate_w = jax.nn.softmax(top_vals, axis=-1)
    e_flat = top_idx.reshape(-1).astype(jnp.int32)
    order = jnp.argsort(e_flat, stable=True).astype(jnp.int32)
    e_sorted = e_flat[order]
    counts = jnp.bincount(e_flat, length=N_EXPERTS).astype(jnp.int32)
    padded = ((counts + MOE_TM - 1) // MOE_TM) * MOE_TM
    start = jnp.cumsum(counts) - counts
    p_end = jnp.cumsum(padded)
    p_start = p_end - padded
    dest_sorted = p_start[e_sorted] + (jnp.arange(n_assign, dtype=jnp.int32) - start[e_sorted])
    row_tok = jnp.zeros((MOE_ROWS,), jnp.int32).at[dest_sorted].set(order // TOP_K)
    row_gate = jnp.zeros((MOE_ROWS,), F32).at[dest_sorted].set(gate_w.reshape(-1)[order])
    dest = jnp.zeros((n_assign,), jnp.int32).at[order].set(dest_sorted)
    blk_start = jnp.arange(MOE_BLOCKS, dtype=jnp.int32) * MOE_TM
    blk_expert = jnp.minimum(jnp.searchsorted(p_end, blk_start, side="right"), N_EXPERTS - 1).astype(jnp.int32)
    blk_valid = jnp.clip(counts[blk_expert] - (blk_start - p_start[blk_expert]), 0, MOE_TM).astype(jnp.int32)
    return row_tok, row_gate, dest, blk_expert, blk_valid


def _ffn_norm_kernel(x_ref, y_ref, g2_ref, lng_ref, lnb_ref, o_ref):
    o_ref[...] = _ln(DEEPNORM_ALPHA * x_ref[...] + g2_ref[0] * y_ref[...]) * lng_ref[...] + lnb_ref[...]


def _ffn_norm(x1, y, mod3, ln_g3, ln_b3, l):
    tm = 512
    row = functools.partial(_mod_row, tm=tm)
    tile = pl.BlockSpec((tm, D_MODEL), lambda i: (i, 0))
    vec = pl.BlockSpec((None, 1, D_MODEL), lambda i: (l, 0, 0))
    return pl.pallas_call(
        _ffn_norm_kernel,
        grid=(T_ALL // tm,),
        in_specs=[tile, tile, pl.BlockSpec((1, 1, D_MODEL), lambda i: (row(i), 0, 5)), vec, vec],
        out_specs=tile,
        out_shape=jax.ShapeDtypeStruct((T_ALL, D_MODEL), F32),
        compiler_params=_params(("parallel",), 40),
        name="ffn_norm",
    )(x1, y, mod3, ln_g3, ln_b3)


def _rope_tables():
    rows = DEC_SEQ // GRID_W
    r, col = jnp.meshgrid(jnp.arange(rows, dtype=F32), jnp.arange(GRID_W, dtype=F32), indexing="ij")
    freqs = ROPE_BASE ** (-jnp.arange(ROT_HALF, dtype=F32) / ROT_HALF)
    ar = r.reshape(-1)[:, None] * freqs
    ac = col.reshape(-1)[:, None] * freqs
    ang = jnp.concatenate([ar, ar, ac, ac] * 2, axis=-1)
    return jnp.cos(ang), jnp.sin(ang)


def kernel(x_prompt, x_sample, c, cache_k, cache_v, state_hgrn, c_ctx, w_mod, b_mod, w_in, conv_w, conv_b, conv_norm_g, conv_norm_b, w_conv_out, lambda_q1, lambda_k1, lambda_q2, lambda_k2, subln_w, w_attn_out, lb_param, hgrn_norm_w, w_hgrn_out, w_o, ln1_g, ln1_b, ln2_g, ln2_b, w_router, b_router, w1, b1, w2, b2):
    lb_all = jnp.cumsum(jax.nn.softmax(lb_param.astype(F32), axis=0), axis=0)
    lb_all = lb_all - lb_all[:1]
    cos, sin = _rope_tables()
    cond = jnp.concatenate([c_ctx[None, :], c, jnp.zeros((N_MOD_ROWS - 1 - DEC_BATCH, D_MODEL), F32)], axis=0)
    row3 = lambda a: a.reshape(DEPTH, 1, a.shape[-1])
    b_mod3, conv_b3, conv_g3, conv_beta3 = row3(b_mod), row3(conv_b), row3(conv_norm_g), row3(conv_norm_b)
    subln3, nw3, b_router3 = row3(subln_w), row3(hgrn_norm_w), row3(b_router)
    ln1_g3, ln1_b3, ln2_g3, ln2_b3 = row3(ln1_g), row3(ln1_b), row3(ln2_g), row3(ln2_b)
    b1_4 = b1.reshape(DEPTH, N_EXPERTS, 1, 2 * D_FF)
    b2_4 = b2.reshape(DEPTH, N_EXPERTS, 1, D_MODEL)

    x = jnp.concatenate([x_prompt.reshape(T_CTX, D_MODEL), x_sample.reshape(T_LAT, D_MODEL)], axis=0)
    ks_new, vs_new, ss_new = [], [], []
    for l in range(DEPTH):
        lam_init = 0.8 - 0.6 * math.exp(-0.3 * l)
        lam = (jnp.exp(jnp.sum(lambda_q1[l].astype(F32) * lambda_k1[l].astype(F32)))
               - jnp.exp(jnp.sum(lambda_q2[l].astype(F32) * lambda_k2[l].astype(F32))) + lam_init).reshape(1, 1)

        mod3 = _modulation(cond, w_mod, b_mod3, l).reshape(N_MOD_ROWS, 1, 6 * D_MODEL)
        proj = _in_proj(x, mod3, w_in, l)

        conv_args = (proj, conv_w, conv_b3, conv_g3, conv_beta3, l)
        ca = jnp.concatenate([_conv_branch(*conv_args, BATCH, SEQ, 0),
                              _conv_branch(*conv_args, DEC_BATCH, DEC_SEQ, T_CTX)], axis=0)
        ob_ctx, k_new, v_new = _attn_branch(proj, lam, subln3, l, BATCH, SEQ, 0)
        ob_lat = _attn_branch(proj, lam, subln3, l, DEC_BATCH, DEC_SEQ, T_CTX, ctx=(cache_k, cache_v, cos, sin))
        oc_ctx, s_new = _hgrn_branch(proj, lb_all, nw3, l, BATCH, SEQ, 0)
        oc_lat = _hgrn_branch(proj, lb_all, nw3, l, DEC_BATCH, DEC_SEQ, T_CTX, state=state_hgrn)
        ks_new.append(k_new)
        vs_new.append(v_new)
        ss_new.append(s_new)
        ob = jnp.concatenate([ob_ctx, ob_lat], axis=0)
        oc = jnp.concatenate([oc_ctx, oc_lat], axis=0)

        x1, u2, logits = _mix(ca, ob, oc, proj, x, mod3, w_conv_out, w_attn_out, w_hgrn_out, w_o,
                              ln1_g3, ln1_b3, w_router, b_router3, l)

        row_tok, row_gate, dest, blk_expert, blk_valid = _route(logits)
        out_rows = _moe(u2[row_tok], row_gate[:, None], blk_expert, blk_valid, w1, b1_4, w2, b2_4, l)
        y = jnp.sum(out_rows[dest].reshape(T_ALL, TOP_K, D_MODEL), axis=1)
        x = _ffn_norm(x1, y, mod3, ln2_g3, ln2_b3, l)

    y_prompt = x[:T_CTX].reshape(BATCH, SEQ, D_MODEL)
    y_sample = x[T_CTX:].reshape(DEC_BATCH, DEC_SEQ, D_MODEL)
    return (y_prompt, y_sample, jnp.stack(ks_new, axis=1), jnp.stack(vs_new, axis=1),
            jnp.stack(ss_new, axis=1))
```

```python
import functools
import math

import jax
import jax.numpy as jnp
from jax import lax
from jax.experimental import pallas as pl
from jax.experimental.pallas import tpu as pltpu

F32 = jnp.float32
BF16 = jnp.bfloat16

D_MODEL = 2048
BATCH = 32
SEQ = 256
DEPTH = 2
DEC_BATCH = 4
DEC_SEQ = 1024
PAST_LEN = 512
GRID_W = 64
D_CONV = D_MODEL // 4
CONV_WIDTH = 31
N_DIFF_HEADS = 8
D_QK = 64
D_V = 2 * D_QK
ROPE_BASE = 10000.0
ROT_HALF = D_QK // 4
HG_HEADS = 4
HG_F = 128
HG_I = (D_MODEL // 4) // HG_HEADS
N_EXPERTS = 32
TOP_K = 4
D_FF = D_MODEL
SWIGLU_ALPHA = 1.702
SWIGLU_LIMIT = 7.0
DEEPNORM_ALPHA = (2 * DEPTH) ** 0.25
LN_EPS = 1e-5

T_CTX = BATCH * SEQ
T_LAT = DEC_BATCH * DEC_SEQ
T_ALL = T_CTX + T_LAT
N_ASSIGN = T_ALL * TOP_K
N_MOD_ROWS = 8

COL_Q = (2 * D_CONV) // 128
COL_K = COL_Q + N_DIFF_HEADS
COL_V = COL_K + N_DIFF_HEADS
COL_HQ = COL_V + N_DIFF_HEADS
COL_HF_FWD = COL_HQ + HG_HEADS
COL_HF_BWD = COL_HF_FWD + HG_HEADS
COL_HI = COL_HF_BWD + HG_HEADS
COL_HG = COL_HI + HG_HEADS
COL_GATES = COL_HG + HG_HEADS
D_IN = COL_GATES * 128 + 3 * D_MODEL

HG_CHUNK = 16
HG_GROUP = 128
ATTN_TQ = 256
CONV_ROWS = 64
CONV_PAD = 16
MOE_TM = 1024
MOE_SUB = 256
MOE_TN = 512
MOE_BLOCKS = N_ASSIGN // MOE_TM + N_EXPERTS
MOE_ROWS = MOE_BLOCKS * MOE_TM
DMA_UNROLL = 8
COMBINE_TM = 128

MIB = 1024 * 1024


def _params(sem, vmem_mib):
    return pltpu.CompilerParams(dimension_semantics=sem, vmem_limit_bytes=vmem_mib * MIB)


def _mod_row(i, tm):
    r0 = i * tm
    return jnp.where(r0 < T_CTX, 0, 1 + (r0 - T_CTX) // DEC_SEQ)


def _ln(x):
    mu = jnp.mean(x, axis=-1, keepdims=True)
    xc = x - mu
    var = jnp.mean(xc * xc, axis=-1, keepdims=True)
    return xc * lax.rsqrt(var + LN_EPS)


def _bdot(a, b):
    return jnp.dot(a.astype(BF16), b.astype(BF16), preferred_element_type=F32)


_ANY = pl.BlockSpec(memory_space=pl.ANY)


def _mod_kernel(c_ref, w_ref, b_ref, o_ref):
    c = c_ref[...]
    o_ref[...] = _bdot(c * jax.nn.sigmoid(c), w_ref[...]) + b_ref[...]


def _modulation(cond, w_mod, b_mod3, l):
    tn = 1024
    return pl.pallas_call(
        _mod_kernel,
        grid=(6 * D_MODEL // tn,),
        in_specs=[
            pl.BlockSpec((N_MOD_ROWS, D_MODEL), lambda j: (0, 0)),
            pl.BlockSpec((None, D_MODEL, tn), lambda j: (l, 0, j)),
            pl.BlockSpec((None, 1, tn), lambda j: (l, 0, j)),
        ],
        out_specs=pl.BlockSpec((N_MOD_ROWS, tn), lambda j: (0, j)),
        out_shape=jax.ShapeDtypeStruct((N_MOD_ROWS, 6 * D_MODEL), F32),
        compiler_params=_params(("arbitrary",), 40),
        name="modulation",
    )(cond, w_mod, b_mod3)


def _in_proj_kernel(x_ref, sh_ref, sc_ref, w_ref, o_ref, u_ref):
    @pl.when(pl.program_id(1) == 0)
    def _():
        u_ref[...] = (_ln(x_ref[...]) * (1.0 + sc_ref[0]) + sh_ref[0]).astype(BF16)

    o_ref[...] = jnp.dot(u_ref[...], w_ref[...].astype(BF16), preferred_element_type=F32)


def _in_proj(x, mod3, w_in, l):
    tm, tn = 1024, 512
    row = functools.partial(_mod_row, tm=tm)
    return pl.pallas_call(
        _in_proj_kernel,
        grid=(T_ALL // tm, D_IN // tn),
        in_specs=[
            pl.BlockSpec((tm, D_MODEL), lambda i, j: (i, 0)),
            pl.BlockSpec((1, 1, D_MODEL), lambda i, j: (row(i), 0, 0)),
            pl.BlockSpec((1, 1, D_MODEL), lambda i, j: (row(i), 0, 1)),
            pl.BlockSpec((None, D_MODEL, tn), lambda i, j: (l, 0, j)),
        ],
        out_specs=pl.BlockSpec((tm, tn), lambda i, j: (i, j)),
        out_shape=jax.ShapeDtypeStruct((T_ALL, D_IN), F32),
        scratch_shapes=[pltpu.VMEM((tm, D_MODEL), BF16)],
        compiler_params=_params(("parallel", "arbitrary"), 48),
        name="in_proj",
    )(x, mod3, mod3, w_in)


def _conv_kernel(glu_ref, w_ref, b_ref, g_ref, beta_ref, o_ref, pad_ref, *, n):
    a = glu_ref[:, :D_CONV]
    g = glu_ref[:, D_CONV:]
    zeros = jnp.zeros((CONV_PAD, D_CONV), F32)
    pad_ref[0:CONV_PAD, :] = zeros
    pad_ref[CONV_PAD + n:CONV_PAD + n + CONV_PAD, :] = zeros
    pad_ref[CONV_PAD:CONV_PAD + n, :] = a * jax.nn.sigmoid(g)
    first = CONV_PAD - CONV_WIDTH // 2
    for r0 in range(0, n, CONV_ROWS):
        acc = jnp.zeros((CONV_ROWS, D_CONV), F32)
        for j in range(CONV_WIDTH):
            s = r0 + first + j
            acc = acc + w_ref[j:j + 1, :] * pad_ref[s:s + CONV_ROWS, :]
        y = _ln(acc + b_ref[...]) * g_ref[...] + beta_ref[...]
        o_ref[r0:r0 + CONV_ROWS, :] = (y * jax.nn.sigmoid(y)).astype(BF16)


def _conv_branch(proj, conv_w, conv_b3, conv_g3, conv_beta3, l, nb, n, row_off):
    off = row_off // n
    vec = pl.BlockSpec((None, 1, D_CONV), lambda b: (l, 0, 0))
    return pl.pallas_call(
        functools.partial(_conv_kernel, n=n),
        grid=(nb,),
        in_specs=[
            pl.BlockSpec((n, 2 * D_CONV), lambda b: (off + b, 0)),
            pl.BlockSpec((None, CONV_WIDTH, D_CONV), lambda b: (l, 0, 0)),
            vec, vec, vec,
        ],
        out_specs=pl.BlockSpec((n, D_CONV), lambda b: (b, 0)),
        out_shape=jax.ShapeDtypeStruct((nb * n, D_CONV), BF16),
        scratch_shapes=[pltpu.VMEM((n + 2 * CONV_PAD, D_CONV), F32)],
        compiler_params=_params(("parallel",), 40),
        name="conv_branch",
    )(proj, conv_w, conv_b3, conv_g3, conv_beta3)


def _rope(x, cos, sin):
    lane = lax.broadcasted_iota(jnp.int32, x.shape, 1)
    nxt = pltpu.roll(x, 128 - ROT_HALF, 1)
    prv = pltpu.roll(x, ROT_HALF, 1)
    rot = jnp.where((lane % (2 * ROT_HALF)) < ROT_HALF, -nxt, prv)
    return x * cos + rot * sin


def _attn_kernel(*refs, n, latent, lam_init):
    if latent:
        lam_ref, q_ref, k_ref, v_ref, sw_ref, ck_ref, cv_ref, cos_ref, sin_ref, o_ref = refs
    else:
        lam_ref, q_ref, k_ref, v_ref, sw_ref, o_ref, ko_ref, vo_ref = refs
    lam = lam_ref[0, 0]
    scale = D_QK ** -0.5
    k = k_ref[...]
    v = v_ref[...]
    if latent:
        k = _rope(k, cos_ref[...], sin_ref[...])
        ckb = ck_ref[...].astype(BF16)
        cvb = cv_ref[...].astype(BF16)
    else:
        ko_ref[...] = k
        vo_ref[...] = v
    kb = k.astype(BF16)
    vb = v.astype(BF16)
    nt = (((1,), (1,)), ((), ()))
    first_sub = lax.broadcasted_iota(jnp.int32, (1, 2 * D_QK), 1) < D_QK

    def softmax_maps(qx):
        s = lax.dot_general(qx, kb, nt, preferred_element_type=F32) * scale
        m = jnp.max(s, axis=-1, keepdims=True)
        if latent:
            sc = lax.dot_general(qx, ckb, nt, preferred_element_type=F32) * scale
            m = jnp.maximum(m, jnp.max(sc, axis=-1, keepdims=True))
            e = jnp.exp(s - m)
            ec = jnp.exp(sc - m)
            den = jnp.sum(e, axis=-1, keepdims=True) + jnp.sum(ec, axis=-1, keepdims=True)
            return e / den, ec / den
        e = jnp.exp(s - m)
        return e / jnp.sum(e, axis=-1, keepdims=True), None

    for r0 in range(0, n, ATTN_TQ):
        q = q_ref[r0:r0 + ATTN_TQ, :]
        if latent:
            q = _rope(q, cos_ref[r0:r0 + ATTN_TQ, :], sin_ref[r0:r0 + ATTN_TQ, :])
        p1, pc1 = softmax_maps(jnp.where(first_sub, q, 0.0).astype(BF16))
        p2, pc2 = softmax_maps(jnp.where(first_sub, 0.0, q).astype(BF16))
        o = _bdot(p1 - lam * p2, vb)
        if latent:
            o = o + _bdot(pc1 - lam * pc2, cvb)
        o = o * lax.rsqrt(jnp.mean(o * o, axis=-1, keepdims=True) + LN_EPS) * sw_ref[...]
        o_ref[r0:r0 + ATTN_TQ, :] = (o * (1.0 - lam_init)).astype(BF16)


def _attn_branch(proj, lam, subln3, l, nb, n, row_off, ctx=None):
    off = row_off // n
    latent = ctx is not None
    lam_init = 0.8 - 0.6 * math.exp(-0.3 * l)
    tok = lambda col: pl.BlockSpec((n, 128), lambda b, h: (off + b, col + h))
    in_specs = [pl.BlockSpec(memory_space=pltpu.SMEM), tok(COL_Q), tok(COL_K), tok(COL_V),
                pl.BlockSpec((None, 1, D_V), lambda b, h: (l, 0, 0))]
    args = [lam, proj, proj, proj, subln3]
    o_spec = pl.BlockSpec((n, D_V), lambda b, h: (b, h))
    o_shape = jax.ShapeDtypeStruct((nb * n, N_DIFF_HEADS * D_V), BF16)
    if latent:
        cache_k, cache_v, cos, sin = ctx
        cache = pl.BlockSpec((None, None, None, PAST_LEN, 128), lambda b, h: (b, l, h, 0, 0))
        table = pl.BlockSpec((n, 128), lambda b, h: (0, 0))
        in_specs += [cache, cache, table, table]
        args += [cache_k, cache_v, cos, sin]
        out_specs, out_shape = o_spec, o_shape
    else:
        kv_spec = pl.BlockSpec((None, None, n, 128), lambda b, h: (b, h, 0, 0))
        kv_shape = jax.ShapeDtypeStruct((nb, N_DIFF_HEADS, n, 128), F32)
        out_specs, out_shape = [o_spec, kv_spec, kv_spec], [o_shape, kv_shape, kv_shape]
    return pl.pallas_call(
        functools.partial(_attn_kernel, n=n, latent=latent, lam_init=lam_init),
        grid=(nb, N_DIFF_HEADS),
        in_specs=in_specs,
        out_specs=out_specs,
        out_shape=out_shape,
        compiler_params=_params(("parallel", "parallel"), 40),
        name="attn_latent" if latent else "attn_context",
    )(*args)


def _hgrn_kernel(*refs, n, has_state):
    if has_state:
        (hq_ref, ff_ref, fb_ref, hi_ref, hg_ref, lb_ref, nw_ref, s0_ref, o_ref,
         qd_scr, kd_scr, ee_scr, o_scr, st_scr) = refs
    else:
        (hq_ref, ff_ref, fb_ref, hi_ref, hg_ref, lb_ref, nw_ref, o_ref, so_ref,
         qd_scr, kd_scr, ee_scr, o_scr, st_scr) = refs
    C, G = HG_CHUNK, HG_GROUP
    nc = n // C
    hq = hq_ref[...]
    q = hq * jax.nn.sigmoid(hq)
    q3 = q.reshape(nc, C, HG_F)
    vb = hi_ref[...].astype(BF16)
    srow = lax.broadcasted_iota(jnp.int32, (n, 1), 0)
    s_loc = srow % C
    lane = lax.broadcasted_iota(jnp.int32, (n, G), 1)
    lane_base = ((srow % G) // C) * C
    gr = lax.broadcasted_iota(jnp.int32, (G, G), 0)
    gc = lax.broadcasted_iota(jnp.int32, (G, G), 1)
    same_chunk = (gr // C) == (gc // C)
    tn = (((0,), (0,)), ((), ()))
    nt = (((1,), (1,)), ((), ()))
    chunk_rows = lambda a3: jnp.broadcast_to(a3, (nc, C, HG_F)).reshape(n, HG_F)

    for d, f_ref in enumerate((ff_ref, fb_ref)):
        fwd = d == 0
        lb = lb_ref[d:d + 1, :]
        f = lb + (1.0 - lb) * jax.nn.sigmoid(f_ref[...])
        lf = jnp.log(f)
        k = 1.0 - f
        tri = (same_chunk & ((gc <= gr) if fwd else (gc >= gr))).astype(F32)
        cum = jnp.concatenate(
            [jnp.dot(tri, lf[g:g + G, :], precision=lax.Precision.HIGHEST, preferred_element_type=F32)
             for g in range(0, n, G)], axis=0)
        cum3 = cum.reshape(nc, C, HG_F)
        edge = chunk_rows(cum3[:, C - 1:C, :] if fwd else cum3[:, 0:1, :])
        qd_scr[d] = (q * jnp.exp(cum)).astype(BF16)
        kd_scr[d] = (k * jnp.exp(edge - cum)).astype(BF16)
        ee_scr[d] = jnp.exp(edge)
        scores_t = jnp.zeros((n, G), F32)
        for t in range(C):
            seen = (s_loc <= t) if fwd else (s_loc >= t)
            rel = jnp.where(seen, chunk_rows(cum3[:, t:t + 1, :]) - cum, -jnp.inf)
            part = jnp.sum(jnp.exp(rel) * k * chunk_rows(q3[:, t:t + 1, :]), axis=-1, keepdims=True)
            scores_t = jnp.where(lane == lane_base + t, part, scores_t)
        sb = scores_t.astype(BF16)
        for g in range(0, n, G):
            o_scr[d, g:g + G, :] = lax.dot_general(sb[g:g + G, :], vb[g:g + G, :], tn,
                                                   preferred_element_type=F32)
        st_scr[d] = s0_ref[d].T if has_state else jnp.zeros((HG_I, HG_F), F32)

    def step(d, r0):
        st = st_scr[d]
        o_scr[d, pl.ds(r0, C), :] += lax.dot_general(
            qd_scr[d, pl.ds(r0, C), :], st.astype(BF16), nt, preferred_element_type=F32)
        upd = lax.dot_general(hi_ref[pl.ds(r0, C), :].astype(BF16), kd_scr[d, pl.ds(r0, C), :], tn,
                              preferred_element_type=F32)
        st_scr[d] = ee_scr[d, pl.ds(r0, C), :][0:1, :] * st + upd

    def body(c, carry):
        step(0, pl.multiple_of(c * C, C))
        step(1, pl.multiple_of((nc - 1 - c) * C, C))
        return carry

    lax.fori_loop(0, nc, body, 0)

    o = o_scr[0] + o_scr[1]
    o = o * lax.rsqrt(jnp.mean(o * o, axis=-1, keepdims=True) + LN_EPS) * nw_ref[...]
    hg = hg_ref[...]
    o_ref[...] = (o * (hg * jax.nn.sigmoid(hg))).astype(BF16)
    if not has_state:
        so_ref[0] = st_scr[0].T
        so_ref[1] = st_scr[1].T


def _hgrn_branch(proj, lb_all, nw3, l, nb, n, row_off, state=None):
    off = row_off // n
    has_state = state is not None
    tok = lambda col: pl.BlockSpec((n, 128), lambda b, h: (off + b, col + h))
    in_specs = [tok(COL_HQ), tok(COL_HF_FWD), tok(COL_HF_BWD), tok(COL_HI), tok(COL_HG),
                pl.BlockSpec((None, 2, HG_F), lambda b, h: (l, 0, h)),
                pl.BlockSpec((None, 1, HG_I), lambda b, h: (l, 0, 0))]
    args = [proj] * 5 + [lb_all, nw3]
    o_spec = pl.BlockSpec((n, HG_I), lambda b, h: (b, h))
    o_shape = jax.ShapeDtypeStruct((nb * n, HG_HEADS * HG_I), BF16)
    if has_state:
        in_specs.append(pl.BlockSpec((None, None, 2, None, HG_F, HG_I), lambda b, h: (b, l, 0, h, 0, 0)))
        args.append(state)
        out_specs, out_shape = o_spec, o_shape
    else:
        out_specs = [o_spec, pl.BlockSpec((None, 2, None, HG_F, HG_I), lambda b, h: (b, 0, h, 0, 0))]
        out_shape = [o_shape, jax.ShapeDtypeStruct((nb, 2, HG_HEADS, HG_F, HG_I), F32)]
    return pl.pallas_call(
        functools.partial(_hgrn_kernel, n=n, has_state=has_state),
        grid=(nb, HG_HEADS),
        in_specs=in_specs,
        out_specs=out_specs,
        out_shape=out_shape,
        scratch_shapes=[pltpu.VMEM((2, n, HG_F), BF16), pltpu.VMEM((2, n, HG_F), BF16),
                        pltpu.VMEM((2, n, HG_F), F32), pltpu.VMEM((2, n, HG_I), F32),
                        pltpu.VMEM((2, HG_I, HG_F), F32)],
        compiler_params=_params(("parallel", "parallel"), 40),
        name="hgrn_latent" if has_state else "hgrn_context",
    )(*args)


def _mix_kernel(ca_c_ref, ca_l_ref, ob_c_ref, ob_l_ref, oc_c_ref, oc_l_ref, ga_ref, gb_ref, gc_ref,
                wc_ref, wa_ref, wh_ref, wo_ref, o_ref, *, ctx_tiles):
    j = pl.program_id(1)
    is_ctx = pl.program_id(0) < ctx_tiles
    pick = lambda c_ref, l_ref: jnp.where(is_ctx, c_ref[...], l_ref[...])
    merged = (jax.nn.sigmoid(ga_ref[...]) * _bdot(pick(ca_c_ref, ca_l_ref), wc_ref[...])
              + jax.nn.sigmoid(gb_ref[...]) * _bdot(pick(ob_c_ref, ob_l_ref), wa_ref[...])
              + jax.nn.sigmoid(gc_ref[...]) * _bdot(pick(oc_c_ref, oc_l_ref), wh_ref[...]))
    part = _bdot(merged, wo_ref[...])

    @pl.when(j == 0)
    def _():
        o_ref[...] = part

    @pl.when(j > 0)
    def _():
        o_ref[...] += part


def _mix(ca, ob, oc, proj, w_conv_out, w_attn_out, w_hgrn_out, w_o, l):
    tm, tn = 512, 256
    nj = D_MODEL // tn
    gate0 = COL_GATES * 128 // tn
    ctx_tiles = T_CTX // tm
    pair = lambda w: [pl.BlockSpec((tm, w), lambda i, j: (jnp.minimum(i, ctx_tiles - 1), 0)),
                      pl.BlockSpec((tm, w), lambda i, j: (jnp.maximum(i - ctx_tiles, 0), 0))]
    return pl.pallas_call(
        functools.partial(_mix_kernel, ctx_tiles=ctx_tiles),
        grid=(T_ALL // tm, nj),
        in_specs=pair(D_CONV) + pair(N_DIFF_HEADS * D_V) + pair(HG_HEADS * HG_I) + [
            pl.BlockSpec((tm, tn), lambda i, j: (i, gate0 + j)),
            pl.BlockSpec((tm, tn), lambda i, j: (i, gate0 + nj + j)),
            pl.BlockSpec((tm, tn), lambda i, j: (i, gate0 + 2 * nj + j)),
            pl.BlockSpec((None, D_CONV, tn), lambda i, j: (l, 0, j)),
            pl.BlockSpec((None, N_DIFF_HEADS * D_V, tn), lambda i, j: (l, 0, j)),
            pl.BlockSpec((None, HG_HEADS * HG_I, tn), lambda i, j: (l, 0, j)),
            pl.BlockSpec((None, tn, D_MODEL), lambda i, j: (l, j, 0)),
        ],
        out_specs=pl.BlockSpec((tm, D_MODEL), lambda i, j: (i, 0)),
        out_shape=jax.ShapeDtypeStruct((T_ALL, D_MODEL), F32),
        compiler_params=_params(("parallel", "arbitrary"), 48),
        name="mix",
    )(*ca, *ob, *oc, proj, proj, proj, w_conv_out, w_attn_out, w_hgrn_out, w_o)


def _post_mix_kernel(x_ref, m_ref, g1_ref, sh2_ref, sc2_ref, lng_ref, lnb_ref, wr_ref, br_ref,
                     x1_ref, u2_ref, lg_ref):
    x1 = _ln(DEEPNORM_ALPHA * x_ref[...] + g1_ref[0] * m_ref[...]) * lng_ref[...] + lnb_ref[...]
    x1_ref[...] = x1
    u2 = _ln(x1) * (1.0 + sc2_ref[0]) + sh2_ref[0]
    u2_ref[...] = u2
    lg_ref[...] = jnp.dot(u2, wr_ref[...], precision=lax.Precision.HIGHEST,
                          preferred_element_type=F32) + br_ref[...]


def _post_mix(x, mixed, mod3, ln_g3, ln_b3, w_router, b_router3, l):
    tm = 256
    row = functools.partial(_mod_row, tm=tm)
    tile = pl.BlockSpec((tm, D_MODEL), lambda i: (i, 0))
    mod_spec = lambda k: pl.BlockSpec((1, 1, D_MODEL), lambda i: (row(i), 0, k))
    vec = pl.BlockSpec((None, 1, D_MODEL), lambda i: (l, 0, 0))
    return pl.pallas_call(
        _post_mix_kernel,
        grid=(T_ALL // tm,),
        in_specs=[
            tile, tile, mod_spec(2), mod_spec(3), mod_spec(4), vec, vec,
            pl.BlockSpec((None, D_MODEL, N_EXPERTS), lambda i: (l, 0, 0)),
            pl.BlockSpec((None, 1, N_EXPERTS), lambda i: (l, 0, 0)),
        ],
        out_specs=[tile, tile, pl.BlockSpec((tm, N_EXPERTS), lambda i: (i, 0))],
        out_shape=[
            jax.ShapeDtypeStruct((T_ALL, D_MODEL), F32),
            jax.ShapeDtypeStruct((T_ALL, D_MODEL), F32),
            jax.ShapeDtypeStruct((T_ALL, N_EXPERTS), F32),
        ],
        compiler_params=_params(("parallel",), 40),
        name="post_mix",
    )(x, mixed, mod3, mod3, mod3, ln_g3, ln_b3, w_router, b_router3)


def _route(logits):
    top_vals, top_idx = lax.top_k(logits, TOP_K)
    gate_w = jax.nn.softmax(top_vals, axis=-1)
    e_flat = top_idx.reshape(-1).astype(jnp.int32)
    onehot = (e_flat[:, None] == jnp.arange(N_EXPERTS, dtype=jnp.int32)[None, :]).astype(jnp.int32)
    seen = jnp.cumsum(onehot, axis=0)
    counts = seen[-1]
    padded = ((counts + MOE_TM - 1) // MOE_TM) * MOE_TM
    p_end = jnp.cumsum(padded)
    p_start = p_end - padded
    dest = jnp.sum(onehot * (p_start[None, :] + seen - 1), axis=1)
    row_tok = jnp.zeros((MOE_ROWS,), jnp.int32).at[dest].set(
        jnp.arange(N_ASSIGN, dtype=jnp.int32) // TOP_K)
    blk_start = jnp.arange(MOE_BLOCKS, dtype=jnp.int32) * MOE_TM
    blk_expert = jnp.minimum(jnp.sum((p_end[None, :] <= blk_start[:, None]).astype(jnp.int32), axis=1),
                             N_EXPERTS - 1)
    blk_valid = jnp.clip(counts[blk_expert] - (blk_start - p_start[blk_expert]), 0, MOE_TM)
    return gate_w, dest, row_tok, blk_expert.astype(jnp.int32), blk_valid.astype(jnp.int32)


def _gather_kernel(nv_ref, tok_ref, tok_next_ref, u_hbm, o_ref, buf, sem):
    b = pl.program_id(0)
    nb = pl.num_programs(0)

    def row_copy(tok, r, slot):
        return pltpu.make_async_copy(u_hbm.at[pl.ds(tok[0, r], 1)], buf.at[slot, pl.ds(r, 1)],
                                     sem.at[slot, r // MOE_SUB])

    def issue(tok, blk, slot):
        ngroups = (nv_ref[blk] + MOE_SUB - 1) // MOE_SUB * (MOE_SUB // DMA_UNROLL)

        def group(g, carry):
            for u in range(DMA_UNROLL):
                row_copy(tok, g * DMA_UNROLL + u, slot).start()
            return carry

        lax.fori_loop(0, ngroups, group, 0)

    @pl.when(b == 0)
    def _():
        issue(tok_ref, 0, 0)

    @pl.when(b + 1 < nb)
    def _():
        issue(tok_next_ref, b + 1, (b + 1) % 2)

    slot = b % 2
    for s in range(MOE_TM // MOE_SUB):
        rows = pl.ds(s * MOE_SUB, MOE_SUB)

        @pl.when(s * MOE_SUB < nv_ref[b])
        def _():
            def group(g, carry):
                for u in range(DMA_UNROLL):
                    row_copy(tok_ref, s * MOE_SUB + g * DMA_UNROLL + u, slot).wait()
                return carry

            lax.fori_loop(0, MOE_SUB // DMA_UNROLL, group, 0)
            o_ref[rows, :] = buf[slot, rows, :].astype(BF16)

        @pl.when(s * MOE_SUB >= nv_ref[b])
        def _():
            o_ref[rows, :] = jnp.zeros((MOE_SUB, D_MODEL), BF16)


def _gather_rows(u2, row_tok, blk_valid):
    tok2 = row_tok.reshape(MOE_BLOCKS, 1, MOE_TM)
    grid_spec = pltpu.PrefetchScalarGridSpec(
        num_scalar_prefetch=1,
        grid=(MOE_BLOCKS,),
        in_specs=[
            pl.BlockSpec((None, 1, MOE_TM), lambda b, nv: (b, 0, 0), memory_space=pltpu.SMEM),
            pl.BlockSpec((None, 1, MOE_TM), lambda b, nv: (jnp.minimum(b + 1, MOE_BLOCKS - 1), 0, 0),
                         memory_space=pltpu.SMEM),
            _ANY,
        ],
        out_specs=pl.BlockSpec((MOE_TM, D_MODEL), lambda b, nv: (b, 0)),
        scratch_shapes=[pltpu.VMEM((2, MOE_TM, D_MODEL), F32),
                        pltpu.SemaphoreType.DMA((2, MOE_TM // MOE_SUB))],
    )
    return pl.pallas_call(
        _gather_kernel,
        grid_spec=grid_spec,
        out_shape=jax.ShapeDtypeStruct((MOE_ROWS, D_MODEL), BF16),
        compiler_params=_params(("arbitrary",), 40),
        name="moe_gather",
    )(blk_valid, tok2, tok2, u2)


def _moe_kernel(be_ref, nv_ref, x_ref, w1_ref, b1_ref, w2_ref, b2_ref, o_ref, w1b_ref, w2b_ref):
    blk = pl.program_id(0)
    j = pl.program_id(1)
    nvalid = nv_ref[blk]

    @pl.when(j == 0)
    def _():
        o_ref[...] = jnp.zeros_like(o_ref)

    @pl.when(nvalid > 0)
    def _():
        w1b_ref[...] = w1_ref[...].astype(BF16)
        w2b_ref[...] = w2_ref[...].astype(BF16)
        r = lax.broadcasted_iota(jnp.int32, (MOE_TN, MOE_TN // 2), 0)
        c = lax.broadcasted_iota(jnp.int32, (MOE_TN, MOE_TN // 2), 1)
        pick_even = (r == 2 * c).astype(BF16)

        def sub_block(r0):
            h = jnp.dot(x_ref[pl.ds(r0, MOE_SUB), :], w1b_ref[...],
                        preferred_element_type=F32) + b1_ref[...]
            glu = jnp.minimum(h, SWIGLU_LIMIT)
            lin = jnp.clip(pltpu.roll(h, MOE_TN - 1, 1), -SWIGLU_LIMIT, SWIGLU_LIMIT)
            act = glu * jax.nn.sigmoid(SWIGLU_ALPHA * glu) * (lin + 1.0)
            act = jnp.dot(act.astype(BF16), pick_even, preferred_element_type=F32)
            o_ref[pl.ds(r0, MOE_SUB), :] += jnp.dot(act.astype(BF16), w2b_ref[...],
                                                    preferred_element_type=F32)

        nsub = (nvalid + MOE_SUB - 1) // MOE_SUB

        def pair(p, carry):
            r0 = pl.multiple_of(p * (2 * MOE_SUB), 2 * MOE_SUB)
            sub_block(r0)
            sub_block(r0 + MOE_SUB)
            return carry

        lax.fori_loop(0, nsub // 2, pair, 0)

        @pl.when(nsub % 2 == 1)
        def _():
            sub_block(pl.multiple_of((nsub - 1) * MOE_SUB, MOE_SUB))

    @pl.when(j == pl.num_programs(1) - 1)
    def _():
        o_ref[...] = o_ref[...] + b2_ref[...]


def _moe(x_rows, blk_expert, blk_valid, w1, b1_4, w2, b2_4, l):
    nj = 2 * D_FF // MOE_TN
    jj = lambda b, j, nv: jnp.where(nv[b] > 0, j, nj - 1)
    grid_spec = pltpu.PrefetchScalarGridSpec(
        num_scalar_prefetch=2,
        grid=(MOE_BLOCKS, nj),
        in_specs=[
            pl.BlockSpec((MOE_TM, D_MODEL), lambda b, j, be, nv: (b, 0)),
            pl.BlockSpec((None, None, D_MODEL, MOE_TN), lambda b, j, be, nv: (l, be[b], 0, jj(b, j, nv))),
            pl.BlockSpec((None, None, 1, MOE_TN), lambda b, j, be, nv: (l, be[b], 0, jj(b, j, nv))),
            pl.BlockSpec((None, None, MOE_TN // 2, D_MODEL), lambda b, j, be, nv: (l, be[b], jj(b, j, nv), 0)),
            pl.BlockSpec((None, None, 1, D_MODEL), lambda b, j, be, nv: (l, be[b], 0, 0)),
        ],
        out_specs=pl.BlockSpec((MOE_TM, D_MODEL), lambda b, j, be, nv: (b, 0)),
        scratch_shapes=[pltpu.VMEM((D_MODEL, MOE_TN), BF16), pltpu.VMEM((MOE_TN // 2, D_MODEL), BF16)],
    )
    return pl.pallas_call(
        _moe_kernel,
        grid_spec=grid_spec,
        out_shape=jax.ShapeDtypeStruct((MOE_ROWS, D_MODEL), F32),
        compiler_params=_params(("arbitrary", "arbitrary"), 56),
        name="moe",
    )(blk_expert, blk_valid, x_rows, w1, b1_4, w2, b2_4)


def _combine_kernel(dest_ref, dest_next_ref, rows_hbm, gate_ref, x_ref, g2_ref, lng_ref, lnb_ref,
                    o_ref, buf, sem):
    i = pl.program_id(0)
    n = pl.num_programs(0)
    tm = COMBINE_TM

    def row_copy(dest, r, k, slot):
        return pltpu.make_async_copy(rows_hbm.at[pl.ds(dest[0, r * TOP_K + k], 1)],
                                     buf.at[slot, k, pl.ds(r, 1)], sem.at[slot])

    def for_all_rows(fn):
        def group(g, carry):
            for u in range(DMA_UNROLL):
                for k in range(TOP_K):
                    fn(g * DMA_UNROLL + u, k)
            return carry

        lax.fori_loop(0, tm // DMA_UNROLL, group, 0)

    @pl.when(i == 0)
    def _():
        for_all_rows(lambda r, k: row_copy(dest_ref, r, k, 0).start())

    @pl.when(i + 1 < n)
    def _():
        for_all_rows(lambda r, k: row_copy(dest_next_ref, r, k, (i + 1) % 2).start())

    slot = i % 2
    for_all_rows(lambda r, k: row_copy(dest_ref, r, k, slot).wait())
    y = jnp.zeros((tm, D_MODEL), F32)
    for k in range(TOP_K):
        y = y + gate_ref[:, k:k + 1] * buf[slot, k]
    o_ref[...] = _ln(DEEPNORM_ALPHA * x_ref[...] + g2_ref[0] * y) * lng_ref[...] + lnb_ref[...]


def _combine(out_rows, dest, gate_w, x1, mod3, ln_g3, ln_b3, l):
    tm = COMBINE_TM
    nt = T_ALL // tm
    dest2 = dest.reshape(nt, 1, tm * TOP_K)
    row = functools.partial(_mod_row, tm=tm)
    tile = pl.BlockSpec((tm, D_MODEL), lambda i: (i, 0))
    vec = pl.BlockSpec((None, 1, D_MODEL), lambda i: (l, 0, 0))
    return pl.pallas_call(
        _combine_kernel,
        grid=(nt,),
        in_specs=[
            pl.BlockSpec((None, 1, tm * TOP_K), lambda i: (i, 0, 0), memory_space=pltpu.SMEM),
            pl.BlockSpec((None, 1, tm * TOP_K), lambda i: (jnp.minimum(i + 1, nt - 1), 0, 0),
                         memory_space=pltpu.SMEM),
            _ANY,
            pl.BlockSpec((tm, TOP_K), lambda i: (i, 0)),
            tile,
            pl.BlockSpec((1, 1, D_MODEL), lambda i: (row(i), 0, 5)),
            vec, vec,
        ],
        out_specs=tile,
        out_shape=jax.ShapeDtypeStruct((T_ALL, D_MODEL), F32),
        scratch_shapes=[pltpu.VMEM((2, TOP_K, tm, D_MODEL), F32), pltpu.SemaphoreType.DMA((2,))],
        compiler_params=_params(("arbitrary",), 40),
        name="moe_combine",
    )(dest2, dest2, out_rows, gate_w, x1, mod3, ln_g3, ln_b3)


def _rope_tables():
    rows = DEC_SEQ // GRID_W
    r, col = jnp.meshgrid(jnp.arange(rows, dtype=F32), jnp.arange(GRID_W, dtype=F32), indexing="ij")
    freqs = ROPE_BASE ** (-jnp.arange(ROT_HALF, dtype=F32) / ROT_HALF)
    ar = r.reshape(-1)[:, None] * freqs
    ac = col.reshape(-1)[:, None] * freqs
    ang = jnp.concatenate([ar, ar, ac, ac] * 2, axis=-1)
    return jnp.cos(ang), jnp.sin(ang)


def kernel(x_prompt, x_sample, c, cache_k, cache_v, state_hgrn, c_ctx, w_mod, b_mod, w_in, conv_w, conv_b, conv_norm_g, conv_norm_b, w_conv_out, lambda_q1, lambda_k1, lambda_q2, lambda_k2, subln_w, w_attn_out, lb_param, hgrn_norm_w, w_hgrn_out, w_o, ln1_g, ln1_b, ln2_g, ln2_b, w_router, b_router, w1, b1, w2, b2):
    lb_all = jnp.cumsum(jax.nn.softmax(lb_param.astype(F32), axis=0), axis=0)
    lb_all = lb_all - lb_all[:1]
    cos, sin = _rope_tables()
    cond = jnp.concatenate([c_ctx[None, :], c, jnp.zeros((N_MOD_ROWS - 1 - DEC_BATCH, D_MODEL), F32)], axis=0)
    row3 = lambda a: a.reshape(DEPTH, 1, a.shape[-1])
    b_mod3, conv_b3, conv_g3, conv_beta3 = row3(b_mod), row3(conv_b), row3(conv_norm_g), row3(conv_norm_b)
    subln3, nw3, b_router3 = row3(subln_w), row3(hgrn_norm_w), row3(b_router)
    ln1_g3, ln1_b3, ln2_g3, ln2_b3 = row3(ln1_g), row3(ln1_b), row3(ln2_g), row3(ln2_b)
    b1_4 = b1.reshape(DEPTH, N_EXPERTS, 1, 2 * D_FF)
    b2_4 = b2.reshape(DEPTH, N_EXPERTS, 1, D_MODEL)

    x = jnp.concatenate([x_prompt.reshape(T_CTX, D_MODEL), x_sample.reshape(T_LAT, D_MODEL)], axis=0)
    ks_new, vs_new, ss_new = [], [], []
    for l in range(DEPTH):
        lam_init = 0.8 - 0.6 * math.exp(-0.3 * l)
        lam = (jnp.exp(jnp.sum(lambda_q1[l].astype(F32) * lambda_k1[l].astype(F32)))
               - jnp.exp(jnp.sum(lambda_q2[l].astype(F32) * lambda_k2[l].astype(F32))) + lam_init).reshape(1, 1)

        mod3 = _modulation(cond, w_mod, b_mod3, l).reshape(N_MOD_ROWS, 1, 6 * D_MODEL)
        proj = _in_proj(x, mod3, w_in, l)

        conv_args = (proj, conv_w, conv_b3, conv_g3, conv_beta3, l)
        ca = (_conv_branch(*conv_args, BATCH, SEQ, 0), _conv_branch(*conv_args, DEC_BATCH, DEC_SEQ, T_CTX))
        ob_ctx, k_new, v_new = _attn_branch(proj, lam, subln3, l, BATCH, SEQ, 0)
        ob = (ob_ctx, _attn_branch(proj, lam, subln3, l, DEC_BATCH, DEC_SEQ, T_CTX,
                                   ctx=(cache_k, cache_v, cos, sin)))
        oc_ctx, s_new = _hgrn_branch(proj, lb_all, nw3, l, BATCH, SEQ, 0)
        oc = (oc_ctx, _hgrn_branch(proj, lb_all, nw3, l, DEC_BATCH, DEC_SEQ, T_CTX, state=state_hgrn))
        ks_new.append(k_new)
        vs_new.append(v_new)
        ss_new.append(s_new)

        mixed = _mix(ca, ob, oc, proj, w_conv_out, w_attn_out, w_hgrn_out, w_o, l)
        x1, u2, logits = _post_mix(x, mixed, mod3, ln1_g3, ln1_b3, w_router, b_router3, l)

        gate_w, dest, row_tok, blk_expert, blk_valid = _route(logits)
        x_rows = _gather_rows(u2, row_tok, blk_valid)
        out_rows = _moe(x_rows, blk_expert, blk_valid, w1, b1_4, w2, b2_4, l)
        x = _combine(out_rows, dest, gate_w, x1, mod3, ln2_g3, ln2_b3, l)

    y_prompt = x[:T_CTX].reshape(BATCH, SEQ, D_MODEL)
    y_sample = x[T_CTX:].reshape(DEC_BATCH, DEC_SEQ, D_MODEL)
    return (y_prompt, y_sample, jnp.stack(ks_new, axis=1), jnp.stack(vs_new, axis=1),
            jnp.stack(ss_new, axis=1))
```

```python
import functools
import math

import jax
import jax.numpy as jnp
from jax import lax
from jax.experimental import pallas as pl
from jax.experimental.pallas import tpu as pltpu

F32 = jnp.float32
BF16 = jnp.bfloat16

D_MODEL = 2048
BATCH = 32
SEQ = 256
DEPTH = 2
DEC_BATCH = 4
DEC_SEQ = 1024
PAST_LEN = 512
GRID_W = 64
D_CONV = D_MODEL // 4
CONV_WIDTH = 31
N_DIFF_HEADS = 8
D_QK = 64
D_V = 2 * D_QK
ROPE_BASE = 10000.0
ROT_HALF = D_QK // 4
HG_HEADS = 4
HG_F = 128
HG_I = (D_MODEL // 4) // HG_HEADS
N_EXPERTS = 32
TOP_K = 4
D_FF = D_MODEL
SWIGLU_ALPHA = 1.702
SWIGLU_LIMIT = 7.0
DEEPNORM_ALPHA = (2 * DEPTH) ** 0.25
LN_EPS = 1e-5

T_CTX = BATCH * SEQ
T_LAT = DEC_BATCH * DEC_SEQ
T_ALL = T_CTX + T_LAT
N_ASSIGN = T_ALL * TOP_K
N_MOD_ROWS = 8

COL_Q = (2 * D_CONV) // 128
COL_K = COL_Q + N_DIFF_HEADS
COL_V = COL_K + N_DIFF_HEADS
COL_HQ = COL_V + N_DIFF_HEADS
COL_HF_FWD = COL_HQ + HG_HEADS
COL_HF_BWD = COL_HF_FWD + HG_HEADS
COL_HI = COL_HF_BWD + HG_HEADS
COL_HG = COL_HI + HG_HEADS
COL_GATES = COL_HG + HG_HEADS
D_IN = COL_GATES * 128 + 3 * D_MODEL

HG_CHUNK = 16
HG_UNROLL = 16
MIX_TM = 512
MIX_TN = 512
HG_GROUP = 128
ATTN_TQ = 256
CONV_ROWS = 64
CONV_PAD = 16
MOE_TM = 1024
MOE_SUB = 256
MOE_TN = 512
MOE_BLOCKS = N_ASSIGN // MOE_TM + N_EXPERTS
MOE_ROWS = MOE_BLOCKS * MOE_TM
DMA_UNROLL = 8
COMBINE_TM = 128

MIB = 1024 * 1024


def _params(sem, vmem_mib):
    return pltpu.CompilerParams(dimension_semantics=sem, vmem_limit_bytes=vmem_mib * MIB)


def _mod_row(i, tm):
    r0 = i * tm
    return jnp.where(r0 < T_CTX, 0, 1 + (r0 - T_CTX) // DEC_SEQ)


def _ln(x):
    mu = jnp.mean(x, axis=-1, keepdims=True)
    xc = x - mu
    var = jnp.mean(xc * xc, axis=-1, keepdims=True)
    return xc * lax.rsqrt(var + LN_EPS)


def _bdot(a, b):
    return jnp.dot(a.astype(BF16), b.astype(BF16), preferred_element_type=F32)


_ANY = pl.BlockSpec(memory_space=pl.ANY)


def _mod_kernel(c_ref, w_ref, b_ref, o_ref):
    c = c_ref[...]
    o_ref[...] = _bdot(c * jax.nn.sigmoid(c), w_ref[...]) + b_ref[...]


def _modulation(cond, w_mod, b_mod3, l):
    tn = 1024
    return pl.pallas_call(
        _mod_kernel,
        grid=(6 * D_MODEL // tn,),
        in_specs=[
            pl.BlockSpec((N_MOD_ROWS, D_MODEL), lambda j: (0, 0)),
            pl.BlockSpec((None, D_MODEL, tn), lambda j: (l, 0, j)),
            pl.BlockSpec((None, 1, tn), lambda j: (l, 0, j)),
        ],
        out_specs=pl.BlockSpec((N_MOD_ROWS, tn), lambda j: (0, j)),
        out_shape=jax.ShapeDtypeStruct((N_MOD_ROWS, 6 * D_MODEL), F32),
        compiler_params=_params(("arbitrary",), 40),
        name="modulation",
    )(cond, w_mod, b_mod3)


def _in_proj_kernel(x_ref, sh_ref, sc_ref, w_ref, o_ref, u_ref):
    @pl.when(pl.program_id(1) == 0)
    def _():
        u_ref[...] = (_ln(x_ref[...]) * (1.0 + sc_ref[0]) + sh_ref[0]).astype(BF16)

    o_ref[...] = jnp.dot(u_ref[...], w_ref[...].astype(BF16), preferred_element_type=F32)


def _in_proj(x, mod3, w_in, l):
    tm, tn = 1024, 512
    row = functools.partial(_mod_row, tm=tm)
    return pl.pallas_call(
        _in_proj_kernel,
        grid=(T_ALL // tm, D_IN // tn),
        in_specs=[
            pl.BlockSpec((tm, D_MODEL), lambda i, j: (i, 0)),
            pl.BlockSpec((1, 1, D_MODEL), lambda i, j: (row(i), 0, 0)),
            pl.BlockSpec((1, 1, D_MODEL), lambda i, j: (row(i), 0, 1)),
            pl.BlockSpec((None, D_MODEL, tn), lambda i, j: (l, 0, j)),
        ],
        out_specs=pl.BlockSpec((tm, tn), lambda i, j: (i, j)),
        out_shape=jax.ShapeDtypeStruct((T_ALL, D_IN), F32),
        scratch_shapes=[pltpu.VMEM((tm, D_MODEL), BF16)],
        compiler_params=_params(("parallel", "arbitrary"), 48),
        name="in_proj",
    )(x, mod3, mod3, w_in)


def _conv_kernel(glu_ref, w_ref, b_ref, g_ref, beta_ref, o_ref, pad_ref, *, n):
    a = glu_ref[:, :D_CONV]
    g = glu_ref[:, D_CONV:]
    zeros = jnp.zeros((CONV_PAD, D_CONV), F32)
    pad_ref[0:CONV_PAD, :] = zeros
    pad_ref[CONV_PAD + n:CONV_PAD + n + CONV_PAD, :] = zeros
    pad_ref[CONV_PAD:CONV_PAD + n, :] = a * jax.nn.sigmoid(g)
    first = CONV_PAD - CONV_WIDTH // 2
    for r0 in range(0, n, CONV_ROWS):
        acc = jnp.zeros((CONV_ROWS, D_CONV), F32)
        for j in range(CONV_WIDTH):
            s = r0 + first + j
            acc = acc + w_ref[j:j + 1, :] * pad_ref[s:s + CONV_ROWS, :]
        y = _ln(acc + b_ref[...]) * g_ref[...] + beta_ref[...]
        o_ref[r0:r0 + CONV_ROWS, :] = (y * jax.nn.sigmoid(y)).astype(BF16)


def _conv_branch(proj, conv_w, conv_b3, conv_g3, conv_beta3, l, nb, n, row_off):
    off = row_off // n
    vec = pl.BlockSpec((None, 1, D_CONV), lambda b: (l, 0, 0))
    return pl.pallas_call(
        functools.partial(_conv_kernel, n=n),
        grid=(nb,),
        in_specs=[
            pl.BlockSpec((n, 2 * D_CONV), lambda b: (off + b, 0)),
            pl.BlockSpec((None, CONV_WIDTH, D_CONV), lambda b: (l, 0, 0)),
            vec, vec, vec,
        ],
        out_specs=pl.BlockSpec((n, D_CONV), lambda b: (b, 0)),
        out_shape=jax.ShapeDtypeStruct((nb * n, D_CONV), BF16),
        scratch_shapes=[pltpu.VMEM((n + 2 * CONV_PAD, D_CONV), F32)],
        compiler_params=_params(("parallel",), 40),
        name="conv_branch",
    )(proj, conv_w, conv_b3, conv_g3, conv_beta3)


def _rope(x, cos, sin):
    lane = lax.broadcasted_iota(jnp.int32, x.shape, 1)
    nxt = pltpu.roll(x, 128 - ROT_HALF, 1)
    prv = pltpu.roll(x, ROT_HALF, 1)
    rot = jnp.where((lane % (2 * ROT_HALF)) < ROT_HALF, -nxt, prv)
    return x * cos + rot * sin


def _attn_kernel(*refs, n, latent, lam_init):
    if latent:
        lam_ref, q_ref, k_ref, v_ref, sw_ref, ck_ref, cv_ref, cos_ref, sin_ref, o_ref = refs
    else:
        lam_ref, q_ref, k_ref, v_ref, sw_ref, o_ref, ko_ref, vo_ref = refs
    lam = lam_ref[0, 0]
    scale = D_QK ** -0.5
    k = k_ref[...]
    v = v_ref[...]
    if latent:
        k = _rope(k, cos_ref[...], sin_ref[...])
        ckb = ck_ref[...].astype(BF16)
        cvb = cv_ref[...].astype(BF16)
    else:
        ko_ref[...] = k
        vo_ref[...] = v
    kb = k.astype(BF16)
    vb = v.astype(BF16)
    nt = (((1,), (1,)), ((), ()))
    first_sub = lax.broadcasted_iota(jnp.int32, (1, 2 * D_QK), 1) < D_QK

    def softmax_maps(qx):
        s = lax.dot_general(qx, kb, nt, preferred_element_type=F32) * scale
        m = jnp.max(s, axis=-1, keepdims=True)
        if latent:
            sc = lax.dot_general(qx, ckb, nt, preferred_element_type=F32) * scale
            m = jnp.maximum(m, jnp.max(sc, axis=-1, keepdims=True))
            e = jnp.exp(s - m)
            ec = jnp.exp(sc - m)
            den = jnp.sum(e, axis=-1, keepdims=True) + jnp.sum(ec, axis=-1, keepdims=True)
            return e / den, ec / den
        e = jnp.exp(s - m)
        return e / jnp.sum(e, axis=-1, keepdims=True), None

    for r0 in range(0, n, ATTN_TQ):
        q = q_ref[r0:r0 + ATTN_TQ, :]
        if latent:
            q = _rope(q, cos_ref[r0:r0 + ATTN_TQ, :], sin_ref[r0:r0 + ATTN_TQ, :])
        p1, pc1 = softmax_maps(jnp.where(first_sub, q, 0.0).astype(BF16))
        p2, pc2 = softmax_maps(jnp.where(first_sub, 0.0, q).astype(BF16))
        o = _bdot(p1 - lam * p2, vb)
        if latent:
            o = o + _bdot(pc1 - lam * pc2, cvb)
        o = o * lax.rsqrt(jnp.mean(o * o, axis=-1, keepdims=True) + LN_EPS) * sw_ref[...]
        o_ref[r0:r0 + ATTN_TQ, :] = (o * (1.0 - lam_init)).astype(BF16)


def _attn_branch(proj, lam, subln3, l, nb, n, row_off, ctx=None):
    off = row_off // n
    latent = ctx is not None
    lam_init = 0.8 - 0.6 * math.exp(-0.3 * l)
    tok = lambda col: pl.BlockSpec((n, 128), lambda b, h: (off + b, col + h))
    in_specs = [pl.BlockSpec(memory_space=pltpu.SMEM), tok(COL_Q), tok(COL_K), tok(COL_V),
                pl.BlockSpec((None, 1, D_V), lambda b, h: (l, 0, 0))]
    args = [lam, proj, proj, proj, subln3]
    o_spec = pl.BlockSpec((n, D_V), lambda b, h: (b, h))
    o_shape = jax.ShapeDtypeStruct((nb * n, N_DIFF_HEADS * D_V), BF16)
    if latent:
        cache_k, cache_v, cos, sin = ctx
        cache = pl.BlockSpec((None, None, None, PAST_LEN, 128), lambda b, h: (b, l, h, 0, 0))
        table = pl.BlockSpec((n, 128), lambda b, h: (0, 0))
        in_specs += [cache, cache, table, table]
        args += [cache_k, cache_v, cos, sin]
        out_specs, out_shape = o_spec, o_shape
    else:
        kv_spec = pl.BlockSpec((None, None, n, 128), lambda b, h: (b, h, 0, 0))
        kv_shape = jax.ShapeDtypeStruct((nb, N_DIFF_HEADS, n, 128), F32)
        out_specs, out_shape = [o_spec, kv_spec, kv_spec], [o_shape, kv_shape, kv_shape]
    return pl.pallas_call(
        functools.partial(_attn_kernel, n=n, latent=latent, lam_init=lam_init),
        grid=(nb, N_DIFF_HEADS),
        in_specs=in_specs,
        out_specs=out_specs,
        out_shape=out_shape,
        compiler_params=_params(("parallel", "parallel"), 40),
        name="attn_latent" if latent else "attn_context",
    )(*args)


def _hgrn_kernel(*refs, n, has_state):
    if has_state:
        (hq_ref, ff_ref, fb_ref, hi_ref, hg_ref, lb_ref, nw_ref, s0_ref, o_ref,
         qd_scr, kd_scr, ee_scr, o_scr, st_scr) = refs
    else:
        (hq_ref, ff_ref, fb_ref, hi_ref, hg_ref, lb_ref, nw_ref, o_ref, so_ref,
         qd_scr, kd_scr, ee_scr, o_scr, st_scr) = refs
    C, G = HG_CHUNK, HG_GROUP
    nc = n // C
    hq = hq_ref[...]
    q = hq * jax.nn.sigmoid(hq)
    q3 = q.reshape(nc, C, HG_F)
    vb = hi_ref[...].astype(BF16)
    srow = lax.broadcasted_iota(jnp.int32, (n, 1), 0)
    s_loc = srow % C
    lane = lax.broadcasted_iota(jnp.int32, (n, G), 1)
    lane_base = ((srow % G) // C) * C
    gr = lax.broadcasted_iota(jnp.int32, (G, G), 0)
    gc = lax.broadcasted_iota(jnp.int32, (G, G), 1)
    same_chunk = (gr // C) == (gc // C)
    tn = (((0,), (0,)), ((), ()))
    nt = (((1,), (1,)), ((), ()))
    chunk_rows = lambda a3: jnp.broadcast_to(a3, (nc, C, HG_F)).reshape(n, HG_F)

    for d, f_ref in enumerate((ff_ref, fb_ref)):
        fwd = d == 0
        lb = lb_ref[d:d + 1, :]
        f = lb + (1.0 - lb) * jax.nn.sigmoid(f_ref[...])
        lf = jnp.log(f)
        k = 1.0 - f
        tri = (same_chunk & ((gc <= gr) if fwd else (gc >= gr))).astype(F32)
        cum = jnp.concatenate(
            [jnp.dot(tri, lf[g:g + G, :], precision=lax.Precision.HIGHEST, preferred_element_type=F32)
             for g in range(0, n, G)], axis=0)
        cum3 = cum.reshape(nc, C, HG_F)
        edge = chunk_rows(cum3[:, C - 1:C, :] if fwd else cum3[:, 0:1, :])
        qd_scr[d] = (q * jnp.exp(cum)).astype(BF16)
        kd_scr[d] = (k * jnp.exp(edge - cum)).astype(BF16)
        ee_scr[d] = jnp.exp(edge)
        scores_t = jnp.zeros((n, G), F32)
        for t in range(C):
            seen = (s_loc <= t) if fwd else (s_loc >= t)
            rel = jnp.where(seen, chunk_rows(cum3[:, t:t + 1, :]) - cum, -jnp.inf)
            part = jnp.sum(jnp.exp(rel) * k * chunk_rows(q3[:, t:t + 1, :]), axis=-1, keepdims=True)
            scores_t = jnp.where(lane == lane_base + t, part, scores_t)
        sb = scores_t.astype(BF16)
        for g in range(0, n, G):
            o_scr[d, g:g + G, :] = lax.dot_general(sb[g:g + G, :], vb[g:g + G, :], tn,
                                                   preferred_element_type=F32)
        st_scr[d] = s0_ref[d].T if has_state else jnp.zeros((HG_I, HG_F), F32)

    def step(d, r0):
        st = st_scr[d]
        o_scr[d, pl.ds(r0, C), :] += lax.dot_general(
            qd_scr[d, pl.ds(r0, C), :], st.astype(BF16), nt, preferred_element_type=F32)
        upd = lax.dot_general(hi_ref[pl.ds(r0, C), :].astype(BF16), kd_scr[d, pl.ds(r0, C), :], tn,
                              preferred_element_type=F32)
        st_scr[d] = ee_scr[d, pl.ds(r0, C), :][0:1, :] * st + upd

    def body(c, carry):
        step(0, pl.multiple_of(c * C, C))
        step(1, pl.multiple_of((nc - 1 - c) * C, C))
        return carry

    lax.fori_loop(0, nc, body, 0, unroll=HG_UNROLL)

    o = o_scr[0] + o_scr[1]
    o = o * lax.rsqrt(jnp.mean(o * o, axis=-1, keepdims=True) + LN_EPS) * nw_ref[...]
    hg = hg_ref[...]
    o_ref[...] = (o * (hg * jax.nn.sigmoid(hg))).astype(BF16)
    if not has_state:
        so_ref[0] = st_scr[0].T
        so_ref[1] = st_scr[1].T


def _hgrn_branch(proj, lb_all, nw3, l, nb, n, row_off, state=None):
    off = row_off // n
    has_state = state is not None
    tok = lambda col: pl.BlockSpec((n, 128), lambda b, h: (off + b, col + h))
    in_specs = [tok(COL_HQ), tok(COL_HF_FWD), tok(COL_HF_BWD), tok(COL_HI), tok(COL_HG),
                pl.BlockSpec((None, 2, HG_F), lambda b, h: (l, 0, h)),
                pl.BlockSpec((None, 1, HG_I), lambda b, h: (l, 0, 0))]
    args = [proj] * 5 + [lb_all, nw3]
    o_spec = pl.BlockSpec((n, HG_I), lambda b, h: (b, h))
    o_shape = jax.ShapeDtypeStruct((nb * n, HG_HEADS * HG_I), BF16)
    if has_state:
        in_specs.append(pl.BlockSpec((None, None, 2, None, HG_F, HG_I), lambda b, h: (b, l, 0, h, 0, 0)))
        args.append(state)
        out_specs, out_shape = o_spec, o_shape
    else:
        out_specs = [o_spec, pl.BlockSpec((None, 2, None, HG_F, HG_I), lambda b, h: (b, 0, h, 0, 0))]
        out_shape = [o_shape, jax.ShapeDtypeStruct((nb, 2, HG_HEADS, HG_F, HG_I), F32)]
    return pl.pallas_call(
        functools.partial(_hgrn_kernel, n=n, has_state=has_state),
        grid=(nb, HG_HEADS),
        in_specs=in_specs,
        out_specs=out_specs,
        out_shape=out_shape,
        scratch_shapes=[pltpu.VMEM((2, n, HG_F), BF16), pltpu.VMEM((2, n, HG_F), BF16),
                        pltpu.VMEM((2, n, HG_F), F32), pltpu.VMEM((2, n, HG_I), F32),
                        pltpu.VMEM((2, HG_I, HG_F), F32)],
        compiler_params=_params(("parallel", "parallel"), 40),
        name="hgrn_latent" if has_state else "hgrn_context",
    )(*args)


def _mix_kernel(ca_c_ref, ca_l_ref, ob_c_ref, ob_l_ref, oc_c_ref, oc_l_ref, ga_ref, gb_ref, gc_ref,
                wc_ref, wa_ref, wh_ref, wo_ref, o_ref, *, ctx_tiles):
    j = pl.program_id(1)
    is_ctx = pl.program_id(0) < ctx_tiles
    pick = lambda c_ref, l_ref: jnp.where(is_ctx, c_ref[...], l_ref[...])
    merged = (jax.nn.sigmoid(ga_ref[...]) * _bdot(pick(ca_c_ref, ca_l_ref), wc_ref[...])
              + jax.nn.sigmoid(gb_ref[...]) * _bdot(pick(ob_c_ref, ob_l_ref), wa_ref[...])
              + jax.nn.sigmoid(gc_ref[...]) * _bdot(pick(oc_c_ref, oc_l_ref), wh_ref[...]))
    part = _bdot(merged, wo_ref[...])

    @pl.when(j == 0)
    def _():
        o_ref[...] = part

    @pl.when(j > 0)
    def _():
        o_ref[...] += part


def _mix(ca, ob, oc, proj, w_conv_out, w_attn_out, w_hgrn_out, w_o, l):
    tm, tn = MIX_TM, MIX_TN
    assert (COL_GATES * 128) % tn == 0 and D_MODEL % tn == 0 and T_CTX % tm == 0
    nj = D_MODEL // tn
    gate0 = COL_GATES * 128 // tn
    ctx_tiles = T_CTX // tm
    pair = lambda w: [pl.BlockSpec((tm, w), lambda i, j: (jnp.minimum(i, ctx_tiles - 1), 0)),
                      pl.BlockSpec((tm, w), lambda i, j: (jnp.maximum(i - ctx_tiles, 0), 0))]
    return pl.pallas_call(
        functools.partial(_mix_kernel, ctx_tiles=ctx_tiles),
        grid=(T_ALL // tm, nj),
        in_specs=pair(D_CONV) + pair(N_DIFF_HEADS * D_V) + pair(HG_HEADS * HG_I) + [
            pl.BlockSpec((tm, tn), lambda i, j: (i, gate0 + j)),
            pl.BlockSpec((tm, tn), lambda i, j: (i, gate0 + nj + j)),
            pl.BlockSpec((tm, tn), lambda i, j: (i, gate0 + 2 * nj + j)),
            pl.BlockSpec((None, D_CONV, tn), lambda i, j: (l, 0, j)),
            pl.BlockSpec((None, N_DIFF_HEADS * D_V, tn), lambda i, j: (l, 0, j)),
            pl.BlockSpec((None, HG_HEADS * HG_I, tn), lambda i, j: (l, 0, j)),
            pl.BlockSpec((None, tn, D_MODEL), lambda i, j: (l, j, 0)),
        ],
        out_specs=pl.BlockSpec((tm, D_MODEL), lambda i, j: (i, 0)),
        out_shape=jax.ShapeDtypeStruct((T_ALL, D_MODEL), F32),
        compiler_params=_params(("parallel", "arbitrary"), 48),
        name="mix",
    )(*ca, *ob, *oc, proj, proj, proj, w_conv_out, w_attn_out, w_hgrn_out, w_o)


def _post_mix_kernel(x_ref, m_ref, g1_ref, sh2_ref, sc2_ref, lng_ref, lnb_ref, wr_ref, br_ref,
                     x1_ref, u2_ref, lg_ref):
    x1 = _ln(DEEPNORM_ALPHA * x_ref[...] + g1_ref[0] * m_ref[...]) * lng_ref[...] + lnb_ref[...]
    x1_ref[...] = x1
    u2 = _ln(x1) * (1.0 + sc2_ref[0]) + sh2_ref[0]
    u2_ref[...] = u2
    lg_ref[...] = jnp.dot(u2, wr_ref[...], precision=lax.Precision.HIGHEST,
                          preferred_element_type=F32) + br_ref[...]


def _post_mix(x, mixed, mod3, ln_g3, ln_b3, w_router, b_router3, l):
    tm = 256
    row = functools.partial(_mod_row, tm=tm)
    tile = pl.BlockSpec((tm, D_MODEL), lambda i: (i, 0))
    mod_spec = lambda k: pl.BlockSpec((1, 1, D_MODEL), lambda i: (row(i), 0, k))
    vec = pl.BlockSpec((None, 1, D_MODEL), lambda i: (l, 0, 0))
    return pl.pallas_call(
        _post_mix_kernel,
        grid=(T_ALL // tm,),
        in_specs=[
            tile, tile, mod_spec(2), mod_spec(3), mod_spec(4), vec, vec,
            pl.BlockSpec((None, D_MODEL, N_EXPERTS), lambda i: (l, 0, 0)),
            pl.BlockSpec((None, 1, N_EXPERTS), lambda i: (l, 0, 0)),
        ],
        out_specs=[tile, tile, pl.BlockSpec((tm, N_EXPERTS), lambda i: (i, 0))],
        out_shape=[
            jax.ShapeDtypeStruct((T_ALL, D_MODEL), F32),
            jax.ShapeDtypeStruct((T_ALL, D_MODEL), F32),
            jax.ShapeDtypeStruct((T_ALL, N_EXPERTS), F32),
        ],
        compiler_params=_params(("parallel",), 40),
        name="post_mix",
    )(x, mixed, mod3, mod3, mod3, ln_g3, ln_b3, w_router, b_router3)


def _route(logits):
    top_vals, top_idx = lax.top_k(logits, TOP_K)
    gate_w = jax.nn.softmax(top_vals, axis=-1)
    e_flat = top_idx.reshape(-1).astype(jnp.int32)
    onehot = (e_flat[:, None] == jnp.arange(N_EXPERTS, dtype=jnp.int32)[None, :]).astype(jnp.int32)
    seen = jnp.cumsum(onehot, axis=0)
    counts = seen[-1]
    padded = ((counts + MOE_TM - 1) // MOE_TM) * MOE_TM
    p_end = jnp.cumsum(padded)
    p_start = p_end - padded
    dest = jnp.sum(onehot * (p_start[None, :] + seen - 1), axis=1)
    row_tok = jnp.zeros((MOE_ROWS,), jnp.int32).at[dest].set(
        jnp.arange(N_ASSIGN, dtype=jnp.int32) // TOP_K)
    blk_start = jnp.arange(MOE_BLOCKS, dtype=jnp.int32) * MOE_TM
    blk_expert = jnp.minimum(jnp.sum((p_end[None, :] <= blk_start[:, None]).astype(jnp.int32), axis=1),
                             N_EXPERTS - 1)
    blk_valid = jnp.clip(counts[blk_expert] - (blk_start - p_start[blk_expert]), 0, MOE_TM)
    return gate_w, dest, row_tok, blk_expert.astype(jnp.int32), blk_valid.astype(jnp.int32)


def _gather_kernel(nv_ref, tok_ref, tok_next_ref, u_hbm, o_ref, buf, sem):
    b = pl.program_id(0)
    nb = pl.num_programs(0)

    n_sub = MOE_TM // MOE_SUB

    def row_copy(tok, s, q, slot):
        r = s * MOE_SUB + q
        return pltpu.make_async_copy(u_hbm.at[pl.ds(tok[0, r], 1)], buf.at[slot, pl.ds(r, 1)], sem.at[slot, s])

    def for_sub_block_rows(tok, s, slot, fn):
        def group(g, carry):
            for u in range(DMA_UNROLL):
                fn(row_copy(tok, s, g * DMA_UNROLL + u, slot))
            return carry

        lax.fori_loop(0, MOE_SUB // DMA_UNROLL, group, 0)

    def issue(tok, blk, slot):
        for s in range(n_sub):
            @pl.when(s * MOE_SUB < nv_ref[blk])
            def _():
                for_sub_block_rows(tok, s, slot, lambda copy: copy.start())

    @pl.when(b == 0)
    def _():
        issue(tok_ref, 0, 0)

    for slot in range(2):
        @pl.when(jnp.logical_and(b + 1 < nb, (b + 1) % 2 == slot))
        def _():
            issue(tok_next_ref, b + 1, slot)

    for slot in range(2):
        @pl.when(b % 2 == slot)
        def _():
            for s in range(n_sub):
                rows = pl.ds(s * MOE_SUB, MOE_SUB)

                @pl.when(s * MOE_SUB < nv_ref[b])
                def _():
                    for_sub_block_rows(tok_ref, s, slot, lambda copy: copy.wait())
                    o_ref[rows, :] = buf[slot, rows, :].astype(BF16)

                @pl.when(s * MOE_SUB >= nv_ref[b])
                def _():
                    o_ref[rows, :] = jnp.zeros((MOE_SUB, D_MODEL), BF16)


def _gather_rows(u2, row_tok, blk_valid):
    tok2 = row_tok.reshape(MOE_BLOCKS, 1, MOE_TM)
    grid_spec = pltpu.PrefetchScalarGridSpec(
        num_scalar_prefetch=1,
        grid=(MOE_BLOCKS,),
        in_specs=[
            pl.BlockSpec((None, 1, MOE_TM), lambda b, nv: (b, 0, 0), memory_space=pltpu.SMEM),
            pl.BlockSpec((None, 1, MOE_TM), lambda b, nv: (jnp.minimum(b + 1, MOE_BLOCKS - 1), 0, 0),
                         memory_space=pltpu.SMEM),
            _ANY,
        ],
        out_specs=pl.BlockSpec((MOE_TM, D_MODEL), lambda b, nv: (b, 0)),
        scratch_shapes=[pltpu.VMEM((2, MOE_TM, D_MODEL), F32),
                        pltpu.SemaphoreType.DMA((2, MOE_TM // MOE_SUB))],
    )
    return pl.pallas_call(
        _gather_kernel,
        grid_spec=grid_spec,
        out_shape=jax.ShapeDtypeStruct((MOE_ROWS, D_MODEL), BF16),
        compiler_params=_params(("arbitrary",), 40),
        name="moe_gather",
    )(blk_valid, tok2, tok2, u2)


def _moe_kernel(be_ref, nv_ref, x_ref, w1_ref, b1_ref, w2_ref, b2_ref, o_ref, w1b_ref, w2b_ref):
    blk = pl.program_id(0)
    j = pl.program_id(1)
    nvalid = nv_ref[blk]

    @pl.when(j == 0)
    def _():
        o_ref[...] = jnp.zeros_like(o_ref)

    @pl.when(nvalid > 0)
    def _():
        w1b_ref[...] = w1_ref[...].astype(BF16)
        w2b_ref[...] = w2_ref[...].astype(BF16)
        r = lax.broadcasted_iota(jnp.int32, (MOE_TN, MOE_TN // 2), 0)
        c = lax.broadcasted_iota(jnp.int32, (MOE_TN, MOE_TN // 2), 1)
        pick_even = (r == 2 * c).astype(BF16)

        def sub_block(r0, rows):
            h = jnp.dot(x_ref[pl.ds(r0, rows), :], w1b_ref[...],
                        preferred_element_type=F32) + b1_ref[...]
            glu = jnp.minimum(h, SWIGLU_LIMIT)
            lin = jnp.clip(pltpu.roll(h, MOE_TN - 1, 1), -SWIGLU_LIMIT, SWIGLU_LIMIT)
            act = glu * jax.nn.sigmoid(SWIGLU_ALPHA * glu) * (lin + 1.0)
            act = jnp.dot(act.astype(BF16), pick_even, preferred_element_type=F32)
            o_ref[pl.ds(r0, rows), :] += jnp.dot(act.astype(BF16), w2b_ref[...],
                                                 preferred_element_type=F32)

        nsub = (nvalid + MOE_SUB - 1) // MOE_SUB

        def pair(p, carry):
            sub_block(pl.multiple_of(p * (2 * MOE_SUB), 2 * MOE_SUB), 2 * MOE_SUB)
            return carry

        lax.fori_loop(0, nsub // 2, pair, 0)

        @pl.when(nsub % 2 == 1)
        def _():
            sub_block(pl.multiple_of((nsub - 1) * MOE_SUB, MOE_SUB), MOE_SUB)

    @pl.when(j == pl.num_programs(1) - 1)
    def _():
        o_ref[...] = o_ref[...] + b2_ref[...]


def _moe(x_rows, blk_expert, blk_valid, w1, b1_4, w2, b2_4, l):
    nj = 2 * D_FF // MOE_TN
    jj = lambda b, j, nv: jnp.where(nv[b] > 0, j, nj - 1)
    grid_spec = pltpu.PrefetchScalarGridSpec(
        num_scalar_prefetch=2,
        grid=(MOE_BLOCKS, nj),
        in_specs=[
            pl.BlockSpec((MOE_TM, D_MODEL), lambda b, j, be, nv: (b, 0)),
            pl.BlockSpec((None, None, D_MODEL, MOE_TN), lambda b, j, be, nv: (l, be[b], 0, jj(b, j, nv))),
            pl.BlockSpec((None, None, 1, MOE_TN), lambda b, j, be, nv: (l, be[b], 0, jj(b, j, nv))),
            pl.BlockSpec((None, None, MOE_TN // 2, D_MODEL), lambda b, j, be, nv: (l, be[b], jj(b, j, nv), 0)),
            pl.BlockSpec((None, None, 1, D_MODEL), lambda b, j, be, nv: (l, be[b], 0, 0)),
        ],
        out_specs=pl.BlockSpec((MOE_TM, D_MODEL), lambda b, j, be, nv: (b, 0)),
        scratch_shapes=[pltpu.VMEM((D_MODEL, MOE_TN), BF16), pltpu.VMEM((MOE_TN // 2, D_MODEL), BF16)],
    )
    return pl.pallas_call(
        _moe_kernel,
        grid_spec=grid_spec,
        out_shape=jax.ShapeDtypeStruct((MOE_ROWS, D_MODEL), F32),
        compiler_params=_params(("arbitrary", "arbitrary"), 56),
        name="moe",
    )(blk_expert, blk_valid, x_rows, w1, b1_4, w2, b2_4)


def _combine_kernel(dest_ref, dest_next_ref, rows_hbm, gate_ref, x_ref, g2_ref, lng_ref, lnb_ref,
                    o_ref, buf, sem):
    i = pl.program_id(0)
    n = pl.num_programs(0)
    tm = COMBINE_TM

    def row_copy(dest, r, k, slot):
        return pltpu.make_async_copy(rows_hbm.at[pl.ds(dest[0, r * TOP_K + k], 1)],
                                     buf.at[slot, k, pl.ds(r, 1)], sem.at[slot])

    def for_all_rows(fn):
        def group(g, carry):
            for u in range(DMA_UNROLL):
                for k in range(TOP_K):
                    fn(g * DMA_UNROLL + u, k)
            return carry

        lax.fori_loop(0, tm // DMA_UNROLL, group, 0)

    @pl.when(i == 0)
    def _():
        for_all_rows(lambda r, k: row_copy(dest_ref, r, k, 0).start())

    @pl.when(i + 1 < n)
    def _():
        for_all_rows(lambda r, k: row_copy(dest_next_ref, r, k, (i + 1) % 2).start())

    slot = i % 2
    for_all_rows(lambda r, k: row_copy(dest_ref, r, k, slot).wait())
    y = jnp.zeros((tm, D_MODEL), F32)
    for k in range(TOP_K):
        y = y + gate_ref[:, k:k + 1] * buf[slot, k]
    o_ref[...] = _ln(DEEPNORM_ALPHA * x_ref[...] + g2_ref[0] * y) * lng_ref[...] + lnb_ref[...]


def _combine(out_rows, dest, gate_w, x1, mod3, ln_g3, ln_b3, l):
    tm = COMBINE_TM
    nt = T_ALL // tm
    dest2 = dest.reshape(nt, 1, tm * TOP_K)
    row = functools.partial(_mod_row, tm=tm)
    tile = pl.BlockSpec((tm, D_MODEL), lambda i: (i, 0))
    vec = pl.BlockSpec((None, 1, D_MODEL), lambda i: (l, 0, 0))
    return pl.pallas_call(
        _combine_kernel,
        grid=(nt,),
        in_specs=[
            pl.BlockSpec((None, 1, tm * TOP_K), lambda i: (i, 0, 0), memory_space=pltpu.SMEM),
            pl.BlockSpec((None, 1, tm * TOP_K), lambda i: (jnp.minimum(i + 1, nt - 1), 0, 0),
                         memory_space=pltpu.SMEM),
            _ANY,
            pl.BlockSpec((tm, TOP_K), lambda i: (i, 0)),
            tile,
            pl.BlockSpec((1, 1, D_MODEL), lambda i: (row(i), 0, 5)),
            vec, vec,
        ],
        out_specs=tile,
        out_shape=jax.ShapeDtypeStruct((T_ALL, D_MODEL), F32),
        scratch_shapes=[pltpu.VMEM((2, TOP_K, tm, D_MODEL), F32), pltpu.SemaphoreType.DMA((2,))],
        compiler_params=_params(("arbitrary",), 40),
        name="moe_combine",
    )(dest2, dest2, out_rows, gate_w, x1, mod3, ln_g3, ln_b3)


def _rope_tables():
    rows = DEC_SEQ // GRID_W
    r, col = jnp.meshgrid(jnp.arange(rows, dtype=F32), jnp.arange(GRID_W, dtype=F32), indexing="ij")
    freqs = ROPE_BASE ** (-jnp.arange(ROT_HALF, dtype=F32) / ROT_HALF)
    ar = r.reshape(-1)[:, None] * freqs
    ac = col.reshape(-1)[:, None] * freqs
    ang = jnp.concatenate([ar, ar, ac, ac] * 2, axis=-1)
    return jnp.cos(ang), jnp.sin(ang)


def kernel(x_prompt, x_sample, c, cache_k, cache_v, state_hgrn, c_ctx, w_mod, b_mod, w_in, conv_w, conv_b, conv_norm_g, conv_norm_b, w_conv_out, lambda_q1, lambda_k1, lambda_q2, lambda_k2, subln_w, w_attn_out, lb_param, hgrn_norm_w, w_hgrn_out, w_o, ln1_g, ln1_b, ln2_g, ln2_b, w_router, b_router, w1, b1, w2, b2):
    lb_all = jnp.cumsum(jax.nn.softmax(lb_param.astype(F32), axis=0), axis=0)
    lb_all = lb_all - lb_all[:1]
    cos, sin = _rope_tables()
    cond = jnp.concatenate([c_ctx[None, :], c, jnp.zeros((N_MOD_ROWS - 1 - DEC_BATCH, D_MODEL), F32)], axis=0)
    row3 = lambda a: a.reshape(DEPTH, 1, a.shape[-1])
    b_mod3, conv_b3, conv_g3, conv_beta3 = row3(b_mod), row3(conv_b), row3(conv_norm_g), row3(conv_norm_b)
    subln3, nw3, b_router3 = row3(subln_w), row3(hgrn_norm_w), row3(b_router)
    ln1_g3, ln1_b3, ln2_g3, ln2_b3 = row3(ln1_g), row3(ln1_b), row3(ln2_g), row3(ln2_b)
    b1_4 = b1.reshape(DEPTH, N_EXPERTS, 1, 2 * D_FF)
    b2_4 = b2.reshape(DEPTH, N_EXPERTS, 1, D_MODEL)
    w_in, w_conv_out, w_attn_out, w_hgrn_out, w_o = (
        w.astype(BF16) for w in (w_in, w_conv_out, w_attn_out, w_hgrn_out, w_o))

    x = jnp.concatenate([x_prompt.reshape(T_CTX, D_MODEL), x_sample.reshape(T_LAT, D_MODEL)], axis=0)
    ks_new, vs_new, ss_new = [], [], []
    for l in range(DEPTH):
        lam_init = 0.8 - 0.6 * math.exp(-0.3 * l)
        lam = (jnp.exp(jnp.sum(lambda_q1[l].astype(F32) * lambda_k1[l].astype(F32)))
               - jnp.exp(jnp.sum(lambda_q2[l].astype(F32) * lambda_k2[l].astype(F32))) + lam_init).reshape(1, 1)

        mod3 = _modulation(cond, w_mod, b_mod3, l).reshape(N_MOD_ROWS, 1, 6 * D_MODEL)
        proj = _in_proj(x, mod3, w_in, l)

        conv_args = (proj, conv_w, conv_b3, conv_g3, conv_beta3, l)
        ca = (_conv_branch(*conv_args, BATCH, SEQ, 0), _conv_branch(*conv_args, DEC_BATCH, DEC_SEQ, T_CTX))
        ob_ctx, k_new, v_new = _attn_branch(proj, lam, subln3, l, BATCH, SEQ, 0)
        ob = (ob_ctx, _attn_branch(proj, lam, subln3, l, DEC_BATCH, DEC_SEQ, T_CTX,
                                   ctx=(cache_k, cache_v, cos, sin)))
        oc_ctx, s_new = _hgrn_branch(proj, lb_all, nw3, l, BATCH, SEQ, 0)
        oc = (oc_ctx, _hgrn_branch(proj, lb_all, nw3, l, DEC_BATCH, DEC_SEQ, T_CTX, state=state_hgrn))
        ks_new.append(k_new)
        vs_new.append(v_new)
        ss_new.append(s_new)

        mixed = _mix(ca, ob, oc, proj, w_conv_out, w_attn_out, w_hgrn_out, w_o, l)
        x1, u2, logits = _post_mix(x, mixed, mod3, ln1_g3, ln1_b3, w_router, b_router3, l)

        gate_w, dest, row_tok, blk_expert, blk_valid = _route(logits)
        x_rows = _gather_rows(u2, row_tok, blk_valid)
        out_rows = _moe(x_rows, blk_expert, blk_valid, w1, b1_4, w2, b2_4, l)
        x = _combine(out_rows, dest, gate_w, x1, mod3, ln2_g3, ln2_b3, l)

    y_prompt = x[:T_CTX].reshape(BATCH, SEQ, D_MODEL)
    y_sample = x[T_CTX:].reshape(DEC_BATCH, DEC_SEQ, D_MODEL)
    return (y_prompt, y_sample, jnp.stack(ks_new, axis=1), jnp.stack(vs_new, axis=1),
            jnp.stack(ss_new, axis=1))
```

```python
import functools
import math

import jax
import jax.numpy as jnp
from jax import lax
from jax.experimental import pallas as pl
from jax.experimental.pallas import tpu as pltpu

F32 = jnp.float32
BF16 = jnp.bfloat16

D_MODEL = 2048
BATCH = 32
SEQ = 256
DEPTH = 2
DEC_BATCH = 4
DEC_SEQ = 1024
PAST_LEN = 512
GRID_W = 64
D_CONV = D_MODEL // 4
CONV_WIDTH = 31
N_DIFF_HEADS = 8
D_QK = 64
D_V = 2 * D_QK
ROPE_BASE = 10000.0
ROT_HALF = D_QK // 4
HG_HEADS = 4
HG_F = 128
HG_I = (D_MODEL // 4) // HG_HEADS
N_EXPERTS = 32
TOP_K = 4
D_FF = D_MODEL
SWIGLU_ALPHA = 1.702
SWIGLU_LIMIT = 7.0
DEEPNORM_ALPHA = (2 * DEPTH) ** 0.25
LN_EPS = 1e-5

T_CTX = BATCH * SEQ
T_LAT = DEC_BATCH * DEC_SEQ
T_ALL = T_CTX + T_LAT
N_ASSIGN = T_ALL * TOP_K
N_MOD_ROWS = 8

COL_Q = (2 * D_CONV) // 128
COL_K = COL_Q + N_DIFF_HEADS
COL_V = COL_K + N_DIFF_HEADS
COL_HQ = COL_V + N_DIFF_HEADS
COL_HF_FWD = COL_HQ + HG_HEADS
COL_HF_BWD = COL_HF_FWD + HG_HEADS
COL_HI = COL_HF_BWD + HG_HEADS
COL_HG = COL_HI + HG_HEADS
COL_GATES = COL_HG + HG_HEADS
D_IN = COL_GATES * 128 + 3 * D_MODEL

HG_CHUNK = 16
HG_UNROLL = 16
MIX_TM = 512
MIX_TN = 512
HG_GROUP = 128
ATTN_TQ = 256
ATTN_CTX_HEADS = 8
CONV_ROWS = 64
CONV_PAD = 16
MOE_TM = 1024
MOE_SUB = 256
MOE_TN = 512
MOE_BLOCKS = N_ASSIGN // MOE_TM + N_EXPERTS
MOE_ROWS = MOE_BLOCKS * MOE_TM
DMA_UNROLL = 8
COMBINE_TM = 128

MIB = 1024 * 1024


def _params(sem, vmem_mib):
    return pltpu.CompilerParams(dimension_semantics=sem, vmem_limit_bytes=vmem_mib * MIB)


def _mod_row(i, tm):
    r0 = i * tm
    return jnp.where(r0 < T_CTX, 0, 1 + (r0 - T_CTX) // DEC_SEQ)


def _ln(x):
    mu = jnp.mean(x, axis=-1, keepdims=True)
    xc = x - mu
    var = jnp.mean(xc * xc, axis=-1, keepdims=True)
    return xc * lax.rsqrt(var + LN_EPS)


def _bdot(a, b):
    return jnp.dot(a.astype(BF16), b.astype(BF16), preferred_element_type=F32)


_ANY = pl.BlockSpec(memory_space=pl.ANY)


def _mod_kernel(c_ref, w_ref, b_ref, o_ref):
    c = c_ref[...]
    o_ref[...] = _bdot(c * jax.nn.sigmoid(c), w_ref[...]) + b_ref[...]


def _modulation(cond, w_mod, b_mod3, l):
    tn = 1024
    return pl.pallas_call(
        _mod_kernel,
        grid=(6 * D_MODEL // tn,),
        in_specs=[
            pl.BlockSpec((N_MOD_ROWS, D_MODEL), lambda j: (0, 0)),
            pl.BlockSpec((None, D_MODEL, tn), lambda j: (l, 0, j)),
            pl.BlockSpec((None, 1, tn), lambda j: (l, 0, j)),
        ],
        out_specs=pl.BlockSpec((N_MOD_ROWS, tn), lambda j: (0, j)),
        out_shape=jax.ShapeDtypeStruct((N_MOD_ROWS, 6 * D_MODEL), F32),
        compiler_params=_params(("arbitrary",), 40),
        name="modulation",
    )(cond, w_mod, b_mod3)


def _in_proj_kernel(x_ref, sh_ref, sc_ref, w_ref, o_ref, u_ref):
    @pl.when(pl.program_id(1) == 0)
    def _():
        u_ref[...] = (_ln(x_ref[...]) * (1.0 + sc_ref[0]) + sh_ref[0]).astype(BF16)

    o_ref[...] = jnp.dot(u_ref[...], w_ref[...].astype(BF16), preferred_element_type=F32)


def _in_proj(x, mod3, w_in, l):
    tm, tn = 1024, 512
    row = functools.partial(_mod_row, tm=tm)
    return pl.pallas_call(
        _in_proj_kernel,
        grid=(T_ALL // tm, D_IN // tn),
        in_specs=[
            pl.BlockSpec((tm, D_MODEL), lambda i, j: (i, 0)),
            pl.BlockSpec((1, 1, D_MODEL), lambda i, j: (row(i), 0, 0)),
            pl.BlockSpec((1, 1, D_MODEL), lambda i, j: (row(i), 0, 1)),
            pl.BlockSpec((None, D_MODEL, tn), lambda i, j: (l, 0, j)),
        ],
        out_specs=pl.BlockSpec((tm, tn), lambda i, j: (i, j)),
        out_shape=jax.ShapeDtypeStruct((T_ALL, D_IN), F32),
        scratch_shapes=[pltpu.VMEM((tm, D_MODEL), BF16)],
        compiler_params=_params(("parallel", "arbitrary"), 48),
        name="in_proj",
    )(x, mod3, mod3, w_in)


def _conv_kernel(glu_ref, w_ref, b_ref, g_ref, beta_ref, o_ref, pad_ref, *, n):
    a = glu_ref[:, :D_CONV]
    g = glu_ref[:, D_CONV:]
    zeros = jnp.zeros((CONV_PAD, D_CONV), F32)
    pad_ref[0:CONV_PAD, :] = zeros
    pad_ref[CONV_PAD + n:CONV_PAD + n + CONV_PAD, :] = zeros
    pad_ref[CONV_PAD:CONV_PAD + n, :] = a * jax.nn.sigmoid(g)
    first = CONV_PAD - CONV_WIDTH // 2
    for r0 in range(0, n, CONV_ROWS):
        acc = jnp.zeros((CONV_ROWS, D_CONV), F32)
        for j in range(CONV_WIDTH):
            s = r0 + first + j
            acc = acc + w_ref[j:j + 1, :] * pad_ref[s:s + CONV_ROWS, :]
        y = _ln(acc + b_ref[...]) * g_ref[...] + beta_ref[...]
        o_ref[r0:r0 + CONV_ROWS, :] = (y * jax.nn.sigmoid(y)).astype(BF16)


def _conv_branch(proj, conv_w, conv_b3, conv_g3, conv_beta3, l, nb, n, row_off):
    off = row_off // n
    vec = pl.BlockSpec((None, 1, D_CONV), lambda b: (l, 0, 0))
    return pl.pallas_call(
        functools.partial(_conv_kernel, n=n),
        grid=(nb,),
        in_specs=[
            pl.BlockSpec((n, 2 * D_CONV), lambda b: (off + b, 0)),
            pl.BlockSpec((None, CONV_WIDTH, D_CONV), lambda b: (l, 0, 0)),
            vec, vec, vec,
        ],
        out_specs=pl.BlockSpec((n, D_CONV), lambda b: (b, 0)),
        out_shape=jax.ShapeDtypeStruct((nb * n, D_CONV), BF16),
        scratch_shapes=[pltpu.VMEM((n + 2 * CONV_PAD, D_CONV), F32)],
        compiler_params=_params(("parallel",), 40),
        name="conv_branch",
    )(proj, conv_w, conv_b3, conv_g3, conv_beta3)


def _rope(x, cos, sin):
    lane = lax.broadcasted_iota(jnp.int32, x.shape, 1)
    nxt = pltpu.roll(x, 128 - ROT_HALF, 1)
    prv = pltpu.roll(x, ROT_HALF, 1)
    rot = jnp.where((lane % (2 * ROT_HALF)) < ROT_HALF, -nxt, prv)
    return x * cos + rot * sin


def _attn_kernel(*refs, n, latent, lam_init, heads):
    if latent:
        lam_ref, q_ref, k_ref, v_ref, sw_ref, ck_ref, cv_ref, cos_ref, sin_ref, o_ref = refs
    else:
        lam_ref, q_ref, k_ref, v_ref, sw_ref, o_ref, ko_ref, vo_ref = refs
    lam = lam_ref[0, 0]
    scale = D_QK ** -0.5
    nt = (((1,), (1,)), ((), ()))
    first_sub = lax.broadcasted_iota(jnp.int32, (1, 2 * D_QK), 1) < D_QK

    for hh in range(heads):
        cols = slice(hh * 128, (hh + 1) * 128)
        k = k_ref[:, cols]
        v = v_ref[:, cols]
        if latent:
            k = _rope(k, cos_ref[...], sin_ref[...])
            ckb = ck_ref[...].astype(BF16)
            cvb = cv_ref[...].astype(BF16)
        else:
            ko_ref[hh] = k
            vo_ref[hh] = v
        kb = k.astype(BF16)
        vb = v.astype(BF16)

        def softmax_maps(qx):
            s = lax.dot_general(qx, kb, nt, preferred_element_type=F32) * scale
            m = jnp.max(s, axis=-1, keepdims=True)
            if latent:
                sc = lax.dot_general(qx, ckb, nt, preferred_element_type=F32) * scale
                m = jnp.maximum(m, jnp.max(sc, axis=-1, keepdims=True))
                e = jnp.exp(s - m)
                ec = jnp.exp(sc - m)
                den = jnp.sum(e, axis=-1, keepdims=True) + jnp.sum(ec, axis=-1, keepdims=True)
                return e / den, ec / den
            e = jnp.exp(s - m)
            return e / jnp.sum(e, axis=-1, keepdims=True), None

        for r0 in range(0, n, ATTN_TQ):
            q = q_ref[r0:r0 + ATTN_TQ, cols]
            if latent:
                q = _rope(q, cos_ref[r0:r0 + ATTN_TQ, :], sin_ref[r0:r0 + ATTN_TQ, :])
            p1, pc1 = softmax_maps(jnp.where(first_sub, q, 0.0).astype(BF16))
            p2, pc2 = softmax_maps(jnp.where(first_sub, 0.0, q).astype(BF16))
            o = _bdot(p1 - lam * p2, vb)
            if latent:
                o = o + _bdot(pc1 - lam * pc2, cvb)
            o = o * lax.rsqrt(jnp.mean(o * o, axis=-1, keepdims=True) + LN_EPS) * sw_ref[...]
            o_ref[r0:r0 + ATTN_TQ, cols] = (o * (1.0 - lam_init)).astype(BF16)


def _attn_branch(proj, lam, subln3, l, nb, n, row_off, ctx=None):
    off = row_off // n
    latent = ctx is not None
    heads = 1 if latent else ATTN_CTX_HEADS
    assert COL_Q % heads == 0 and N_DIFF_HEADS % heads == 0
    lam_init = 0.8 - 0.6 * math.exp(-0.3 * l)
    tok = lambda col: pl.BlockSpec((n, heads * 128), lambda b, h: (off + b, col // heads + h))
    in_specs = [pl.BlockSpec(memory_space=pltpu.SMEM), tok(COL_Q), tok(COL_K), tok(COL_V),
                pl.BlockSpec((None, 1, D_V), lambda b, h: (l, 0, 0))]
    args = [lam, proj, proj, proj, subln3]
    o_spec = pl.BlockSpec((n, heads * D_V), lambda b, h: (b, h))
    o_shape = jax.ShapeDtypeStruct((nb * n, N_DIFF_HEADS * D_V), BF16)
    if latent:
        cache_k, cache_v, cos, sin = ctx
        cache = pl.BlockSpec((None, None, None, PAST_LEN, 128), lambda b, h: (b, l, h, 0, 0))
        table = pl.BlockSpec((n, 128), lambda b, h: (0, 0))
        in_specs += [cache, cache, table, table]
        args += [cache_k, cache_v, cos, sin]
        out_specs, out_shape = o_spec, o_shape
    else:
        kv_spec = pl.BlockSpec((None, heads, n, 128), lambda b, h: (b, h, 0, 0))
        kv_shape = jax.ShapeDtypeStruct((nb, N_DIFF_HEADS, n, 128), F32)
        out_specs, out_shape = [o_spec, kv_spec, kv_spec], [o_shape, kv_shape, kv_shape]
    return pl.pallas_call(
        functools.partial(_attn_kernel, n=n, latent=latent, lam_init=lam_init, heads=heads),
        grid=(nb, N_DIFF_HEADS // heads),
        in_specs=in_specs,
        out_specs=out_specs,
        out_shape=out_shape,
        compiler_params=_params(("parallel", "parallel"), 40),
        name="attn_latent" if latent else "attn_context",
    )(*args)


def _hgrn_kernel(*refs, n, has_state):
    if has_state:
        (hq_ref, ff_ref, fb_ref, hi_ref, hg_ref, lb_ref, nw_ref, s0_ref, o_ref,
         qd_scr, kd_scr, ee_scr, o_scr, st_scr) = refs
    else:
        (hq_ref, ff_ref, fb_ref, hi_ref, hg_ref, lb_ref, nw_ref, o_ref, so_ref,
         qd_scr, kd_scr, ee_scr, o_scr, st_scr) = refs
    C, G = HG_CHUNK, HG_GROUP
    nc = n // C
    hq = hq_ref[...]
    q = hq * jax.nn.sigmoid(hq)
    q3 = q.reshape(nc, C, HG_F)
    vb = hi_ref[...].astype(BF16)
    srow = lax.broadcasted_iota(jnp.int32, (n, 1), 0)
    s_loc = srow % C
    lane = lax.broadcasted_iota(jnp.int32, (n, G), 1)
    lane_base = ((srow % G) // C) * C
    gr = lax.broadcasted_iota(jnp.int32, (G, G), 0)
    gc = lax.broadcasted_iota(jnp.int32, (G, G), 1)
    same_chunk = (gr // C) == (gc // C)
    tn = (((0,), (0,)), ((), ()))
    nt = (((1,), (1,)), ((), ()))
    chunk_rows = lambda a3: jnp.broadcast_to(a3, (nc, C, HG_F)).reshape(n, HG_F)

    for d, f_ref in enumerate((ff_ref, fb_ref)):
        fwd = d == 0
        lb = lb_ref[d:d + 1, :]
        f = lb + (1.0 - lb) * jax.nn.sigmoid(f_ref[...])
        lf = jnp.log(f)
        k = 1.0 - f
        tri = (same_chunk & ((gc <= gr) if fwd else (gc >= gr))).astype(F32)
        cum = jnp.concatenate(
            [jnp.dot(tri, lf[g:g + G, :], precision=lax.Precision.HIGHEST, preferred_element_type=F32)
             for g in range(0, n, G)], axis=0)
        cum3 = cum.reshape(nc, C, HG_F)
        edge = chunk_rows(cum3[:, C - 1:C, :] if fwd else cum3[:, 0:1, :])
        qd_scr[d] = (q * jnp.exp(cum)).astype(BF16)
        kd_scr[d] = (k * jnp.exp(edge - cum)).astype(BF16)
        ee_scr[d] = jnp.exp(edge)
        scores_t = jnp.zeros((n, G), F32)
        for t in range(C):
            seen = (s_loc <= t) if fwd else (s_loc >= t)
            rel = jnp.where(seen, chunk_rows(cum3[:, t:t + 1, :]) - cum, -jnp.inf)
            part = jnp.sum(jnp.exp(rel) * k * chunk_rows(q3[:, t:t + 1, :]), axis=-1, keepdims=True)
            scores_t = jnp.where(lane == lane_base + t, part, scores_t)
        sb = scores_t.astype(BF16)
        for g in range(0, n, G):
            o_scr[d, g:g + G, :] = lax.dot_general(sb[g:g + G, :], vb[g:g + G, :], tn,
                                                   preferred_element_type=F32)
        st_scr[d] = s0_ref[d].T if has_state else jnp.zeros((HG_I, HG_F), F32)

    def step(d, r0):
        st = st_scr[d]
        o_scr[d, pl.ds(r0, C), :] += lax.dot_general(
            qd_scr[d, pl.ds(r0, C), :], st.astype(BF16), nt, preferred_element_type=F32)
        upd = lax.dot_general(hi_ref[pl.ds(r0, C), :].astype(BF16), kd_scr[d, pl.ds(r0, C), :], tn,
                              preferred_element_type=F32)
        st_scr[d] = ee_scr[d, pl.ds(r0, C), :][0:1, :] * st + upd

    def body(c, carry):
        step(0, pl.multiple_of(c * C, C))
        step(1, pl.multiple_of((nc - 1 - c) * C, C))
        return carry

    lax.fori_loop(0, nc, body, 0, unroll=HG_UNROLL)

    o = o_scr[0] + o_scr[1]
    o = o * lax.rsqrt(jnp.mean(o * o, axis=-1, keepdims=True) + LN_EPS) * nw_ref[...]
    hg = hg_ref[...]
    o_ref[...] = (o * (hg * jax.nn.sigmoid(hg))).astype(BF16)
    if not has_state:
        so_ref[0] = st_scr[0].T
        so_ref[1] = st_scr[1].T


def _hgrn_branch(proj, lb_all, nw3, l, nb, n, row_off, state=None):
    off = row_off // n
    has_state = state is not None
    tok = lambda col: pl.BlockSpec((n, 128), lambda b, h: (off + b, col + h))
    in_specs = [tok(COL_HQ), tok(COL_HF_FWD), tok(COL_HF_BWD), tok(COL_HI), tok(COL_HG),
                pl.BlockSpec((None, 2, HG_F), lambda b, h: (l, 0, h)),
                pl.BlockSpec((None, 1, HG_I), lambda b, h: (l, 0, 0))]
    args = [proj] * 5 + [lb_all, nw3]
    o_spec = pl.BlockSpec((n, HG_I), lambda b, h: (b, h))
    o_shape = jax.ShapeDtypeStruct((nb * n, HG_HEADS * HG_I), BF16)
    if has_state:
        in_specs.append(pl.BlockSpec((None, None, 2, None, HG_F, HG_I), lambda b, h: (b, l, 0, h, 0, 0)))
        args.append(state)
        out_specs, out_shape = o_spec, o_shape
    else:
        out_specs = [o_spec, pl.BlockSpec((None, 2, None, HG_F, HG_I), lambda b, h: (b, 0, h, 0, 0))]
        out_shape = [o_shape, jax.ShapeDtypeStruct((nb, 2, HG_HEADS, HG_F, HG_I), F32)]
    return pl.pallas_call(
        functools.partial(_hgrn_kernel, n=n, has_state=has_state),
        grid=(nb, HG_HEADS),
        in_specs=in_specs,
        out_specs=out_specs,
        out_shape=out_shape,
        scratch_shapes=[pltpu.VMEM((2, n, HG_F), BF16), pltpu.VMEM((2, n, HG_F), BF16),
                        pltpu.VMEM((2, n, HG_F), F32), pltpu.VMEM((2, n, HG_I), F32),
                        pltpu.VMEM((2, HG_I, HG_F), F32)],
        compiler_params=_params(("parallel", "parallel"), 40),
        name="hgrn_latent" if has_state else "hgrn_context",
    )(*args)


def _mix_kernel(ca_c_ref, ca_l_ref, ob_c_ref, ob_l_ref, oc_c_ref, oc_l_ref, ga_ref, gb_ref, gc_ref,
                wc_ref, wa_ref, wh_ref, wo_ref, o_ref, *, ctx_tiles):
    j = pl.program_id(1)
    is_ctx = pl.program_id(0) < ctx_tiles
    pick = lambda c_ref, l_ref: jnp.where(is_ctx, c_ref[...], l_ref[...])
    merged = (jax.nn.sigmoid(ga_ref[...]) * _bdot(pick(ca_c_ref, ca_l_ref), wc_ref[...])
              + jax.nn.sigmoid(gb_ref[...]) * _bdot(pick(ob_c_ref, ob_l_ref), wa_ref[...])
              + jax.nn.sigmoid(gc_ref[...]) * _bdot(pick(oc_c_ref, oc_l_ref), wh_ref[...]))
    part = _bdot(merged, wo_ref[...])

    @pl.when(j == 0)
    def _():
        o_ref[...] = part

    @pl.when(j > 0)
    def _():
        o_ref[...] += part


def _mix(ca, ob, oc, proj, w_conv_out, w_attn_out, w_hgrn_out, w_o, l):
    tm, tn = MIX_TM, MIX_TN
    assert (COL_GATES * 128) % tn == 0 and D_MODEL % tn == 0 and T_CTX % tm == 0
    nj = D_MODEL // tn
    gate0 = COL_GATES * 128 // tn
    ctx_tiles = T_CTX // tm
    pair = lambda w: [pl.BlockSpec((tm, w), lambda i, j: (jnp.minimum(i, ctx_tiles - 1), 0)),
                      pl.BlockSpec((tm, w), lambda i, j: (jnp.maximum(i - ctx_tiles, 0), 0))]
    return pl.pallas_call(
        functools.partial(_mix_kernel, ctx_tiles=ctx_tiles),
        grid=(T_ALL // tm, nj),
        in_specs=pair(D_CONV) + pair(N_DIFF_HEADS * D_V) + pair(HG_HEADS * HG_I) + [
            pl.BlockSpec((tm, tn), lambda i, j: (i, gate0 + j)),
            pl.BlockSpec((tm, tn), lambda i, j: (i, gate0 + nj + j)),
            pl.BlockSpec((tm, tn), lambda i, j: (i, gate0 + 2 * nj + j)),
            pl.BlockSpec((None, D_CONV, tn), lambda i, j: (l, 0, j)),
            pl.BlockSpec((None, N_DIFF_HEADS * D_V, tn), lambda i, j: (l, 0, j)),
            pl.BlockSpec((None, HG_HEADS * HG_I, tn), lambda i, j: (l, 0, j)),
            pl.BlockSpec((None, tn, D_MODEL), lambda i, j: (l, j, 0)),
        ],
        out_specs=pl.BlockSpec((tm, D_MODEL), lambda i, j: (i, 0)),
        out_shape=jax.ShapeDtypeStruct((T_ALL, D_MODEL), F32),
        compiler_params=_params(("parallel", "arbitrary"), 48),
        name="mix",
    )(*ca, *ob, *oc, proj, proj, proj, w_conv_out, w_attn_out, w_hgrn_out, w_o)


def _post_mix_kernel(x_ref, m_ref, g1_ref, sh2_ref, sc2_ref, lng_ref, lnb_ref, wr_ref, br_ref,
                     x1_ref, u2_ref, lg_ref):
    x1 = _ln(DEEPNORM_ALPHA * x_ref[...] + g1_ref[0] * m_ref[...]) * lng_ref[...] + lnb_ref[...]
    x1_ref[...] = x1
    u2 = _ln(x1) * (1.0 + sc2_ref[0]) + sh2_ref[0]
    u2_ref[...] = u2
    lg_ref[...] = jnp.dot(u2, wr_ref[...], precision=lax.Precision.HIGHEST,
                          preferred_element_type=F32) + br_ref[...]


def _post_mix(x, mixed, mod3, ln_g3, ln_b3, w_router, b_router3, l):
    tm = 256
    row = functools.partial(_mod_row, tm=tm)
    tile = pl.BlockSpec((tm, D_MODEL), lambda i: (i, 0))
    mod_spec = lambda k: pl.BlockSpec((1, 1, D_MODEL), lambda i: (row(i), 0, k))
    vec = pl.BlockSpec((None, 1, D_MODEL), lambda i: (l, 0, 0))
    return pl.pallas_call(
        _post_mix_kernel,
        grid=(T_ALL // tm,),
        in_specs=[
            tile, tile, mod_spec(2), mod_spec(3), mod_spec(4), vec, vec,
            pl.BlockSpec((None, D_MODEL, N_EXPERTS), lambda i: (l, 0, 0)),
            pl.BlockSpec((None, 1, N_EXPERTS), lambda i: (l, 0, 0)),
        ],
        out_specs=[tile, tile, pl.BlockSpec((tm, N_EXPERTS), lambda i: (i, 0))],
        out_shape=[
            jax.ShapeDtypeStruct((T_ALL, D_MODEL), F32),
            jax.ShapeDtypeStruct((T_ALL, D_MODEL), F32),
            jax.ShapeDtypeStruct((T_ALL, N_EXPERTS), F32),
        ],
        compiler_params=_params(("parallel",), 40),
        name="post_mix",
    )(x, mixed, mod3, mod3, mod3, ln_g3, ln_b3, w_router, b_router3)


def _route(logits):
    top_vals, top_idx = lax.top_k(logits, TOP_K)
    gate_w = jax.nn.softmax(top_vals, axis=-1)
    e_flat = top_idx.reshape(-1).astype(jnp.int32)
    onehot = (e_flat[:, None] == jnp.arange(N_EXPERTS, dtype=jnp.int32)[None, :]).astype(jnp.int32)
    seen = jnp.cumsum(onehot, axis=0)
    counts = seen[-1]
    padded = ((counts + MOE_TM - 1) // MOE_TM) * MOE_TM
    p_end = jnp.cumsum(padded)
    p_start = p_end - padded
    dest = jnp.sum(onehot * (p_start[None, :] + seen - 1), axis=1)
    row_tok = jnp.zeros((MOE_ROWS,), jnp.int32).at[dest].set(
        jnp.arange(N_ASSIGN, dtype=jnp.int32) // TOP_K)
    blk_start = jnp.arange(MOE_BLOCKS, dtype=jnp.int32) * MOE_TM
    blk_expert = jnp.minimum(jnp.sum((p_end[None, :] <= blk_start[:, None]).astype(jnp.int32), axis=1),
                             N_EXPERTS - 1)
    blk_valid = jnp.clip(counts[blk_expert] - (blk_start - p_start[blk_expert]), 0, MOE_TM)
    return gate_w, dest, row_tok, blk_expert.astype(jnp.int32), blk_valid.astype(jnp.int32)


def _gather_kernel(nv_ref, tok_ref, tok_next_ref, u_hbm, o_ref, buf, sem):
    b = pl.program_id(0)
    nb = pl.num_programs(0)

    n_sub = MOE_TM // MOE_SUB

    def row_copy(tok, s, q, slot):
        r = s * MOE_SUB + q
        return pltpu.make_async_copy(u_hbm.at[pl.ds(tok[0, r], 1)], buf.at[slot, pl.ds(r, 1)], sem.at[slot, s])

    def for_sub_block_rows(tok, s, slot, fn):
        def group(g, carry):
            for u in range(DMA_UNROLL):
                fn(row_copy(tok, s, g * DMA_UNROLL + u, slot))
            return carry

        lax.fori_loop(0, MOE_SUB // DMA_UNROLL, group, 0)

    def issue(tok, blk, slot):
        for s in range(n_sub):
            @pl.when(s * MOE_SUB < nv_ref[blk])
            def _():
                for_sub_block_rows(tok, s, slot, lambda copy: copy.start())

    @pl.when(b == 0)
    def _():
        issue(tok_ref, 0, 0)

    for slot in range(2):
        @pl.when(jnp.logical_and(b + 1 < nb, (b + 1) % 2 == slot))
        def _():
            issue(tok_next_ref, b + 1, slot)

    for slot in range(2):
        @pl.when(b % 2 == slot)
        def _():
            for s in range(n_sub):
                rows = pl.ds(s * MOE_SUB, MOE_SUB)

                @pl.when(s * MOE_SUB < nv_ref[b])
                def _():
                    for_sub_block_rows(tok_ref, s, slot, lambda copy: copy.wait())
                    o_ref[rows, :] = buf[slot, rows, :].astype(BF16)

                @pl.when(s * MOE_SUB >= nv_ref[b])
                def _():
                    o_ref[rows, :] = jnp.zeros((MOE_SUB, D_MODEL), BF16)


def _gather_rows(u2, row_tok, blk_valid):
    tok2 = row_tok.reshape(MOE_BLOCKS, 1, MOE_TM)
    grid_spec = pltpu.PrefetchScalarGridSpec(
        num_scalar_prefetch=1,
        grid=(MOE_BLOCKS,),
        in_specs=[
            pl.BlockSpec((None, 1, MOE_TM), lambda b, nv: (b, 0, 0), memory_space=pltpu.SMEM),
            pl.BlockSpec((None, 1, MOE_TM), lambda b, nv: (jnp.minimum(b + 1, MOE_BLOCKS - 1), 0, 0),
                         memory_space=pltpu.SMEM),
            _ANY,
        ],
        out_specs=pl.BlockSpec((MOE_TM, D_MODEL), lambda b, nv: (b, 0)),
        scratch_shapes=[pltpu.VMEM((2, MOE_TM, D_MODEL), F32),
                        pltpu.SemaphoreType.DMA((2, MOE_TM // MOE_SUB))],
    )
    return pl.pallas_call(
        _gather_kernel,
        grid_spec=grid_spec,
        out_shape=jax.ShapeDtypeStruct((MOE_ROWS, D_MODEL), BF16),
        compiler_params=_params(("arbitrary",), 40),
        name="moe_gather",
    )(blk_valid, tok2, tok2, u2)


def _moe_kernel(be_ref, nv_ref, x_ref, w1_ref, b1_ref, w2_ref, b2_ref, o_ref, w1b_ref, w2b_ref):
    blk = pl.program_id(0)
    j = pl.program_id(1)
    nvalid = nv_ref[blk]

    @pl.when(j == 0)
    def _():
        o_ref[...] = jnp.zeros_like(o_ref)

    @pl.when(nvalid > 0)
    def _():
        w1b_ref[...] = w1_ref[...].astype(BF16)
        w2b_ref[...] = w2_ref[...].astype(BF16)
        r = lax.broadcasted_iota(jnp.int32, (MOE_TN, MOE_TN // 2), 0)
        c = lax.broadcasted_iota(jnp.int32, (MOE_TN, MOE_TN // 2), 1)
        pick_even = (r == 2 * c).astype(BF16)

        def sub_block(r0, rows):
            h = jnp.dot(x_ref[pl.ds(r0, rows), :], w1b_ref[...],
                        preferred_element_type=F32) + b1_ref[...]
            glu = jnp.minimum(h, SWIGLU_LIMIT)
            lin = jnp.clip(pltpu.roll(h, MOE_TN - 1, 1), -SWIGLU_LIMIT, SWIGLU_LIMIT)
            act = glu * jax.nn.sigmoid(SWIGLU_ALPHA * glu) * (lin + 1.0)
            act = jnp.dot(act.astype(BF16), pick_even, preferred_element_type=F32)
            o_ref[pl.ds(r0, rows), :] += jnp.dot(act.astype(BF16), w2b_ref[...],
                                                 preferred_element_type=F32)

        nsub = (nvalid + MOE_SUB - 1) // MOE_SUB

        @pl.when(nsub == MOE_TM // MOE_SUB)
        def _():
            sub_block(0, MOE_TM)

        @pl.when(nsub < MOE_TM // MOE_SUB)
        def _():
            def pair(p, carry):
                sub_block(pl.multiple_of(p * (2 * MOE_SUB), 2 * MOE_SUB), 2 * MOE_SUB)
                return carry

            lax.fori_loop(0, nsub // 2, pair, 0)

            @pl.when(nsub % 2 == 1)
            def _():
                sub_block(pl.multiple_of((nsub - 1) * MOE_SUB, MOE_SUB), MOE_SUB)

    @pl.when(j == pl.num_programs(1) - 1)
    def _():
        o_ref[...] = o_ref[...] + b2_ref[...]


def _moe(x_rows, blk_expert, blk_valid, w1, b1_4, w2, b2_4, l):
    nj = 2 * D_FF // MOE_TN
    jj = lambda b, j, nv: jnp.where(nv[b] > 0, j, nj - 1)
    grid_spec = pltpu.PrefetchScalarGridSpec(
        num_scalar_prefetch=2,
        grid=(MOE_BLOCKS, nj),
        in_specs=[
            pl.BlockSpec((MOE_TM, D_MODEL), lambda b, j, be, nv: (b, 0)),
            pl.BlockSpec((None, None, D_MODEL, MOE_TN), lambda b, j, be, nv: (l, be[b], 0, jj(b, j, nv))),
            pl.BlockSpec((None, None, 1, MOE_TN), lambda b, j, be, nv: (l, be[b], 0, jj(b, j, nv))),
            pl.BlockSpec((None, None, MOE_TN // 2, D_MODEL), lambda b, j, be, nv: (l, be[b], jj(b, j, nv), 0)),
            pl.BlockSpec((None, None, 1, D_MODEL), lambda b, j, be, nv: (l, be[b], 0, 0)),
        ],
        out_specs=pl.BlockSpec((MOE_TM, D_MODEL), lambda b, j, be, nv: (b, 0)),
        scratch_shapes=[pltpu.VMEM((D_MODEL, MOE_TN), BF16), pltpu.VMEM((MOE_TN // 2, D_MODEL), BF16)],
    )
    return pl.pallas_call(
        _moe_kernel,
        grid_spec=grid_spec,
        out_shape=jax.ShapeDtypeStruct((MOE_ROWS, D_MODEL), F32),
        compiler_params=_params(("arbitrary", "arbitrary"), 56),
        name="moe",
    )(blk_expert, blk_valid, x_rows, w1, b1_4, w2, b2_4)


def _combine_kernel(dest_ref, dest_next_ref, rows_hbm, gate_ref, x_ref, g2_ref, lng_ref, lnb_ref,
                    o_ref, buf, sem):
    i = pl.program_id(0)
    n = pl.num_programs(0)
    tm = COMBINE_TM

    def row_copy(dest, r, k, slot):
        return pltpu.make_async_copy(rows_hbm.at[pl.ds(dest[0, r * TOP_K + k], 1)],
                                     buf.at[slot, k, pl.ds(r, 1)], sem.at[slot])

    def for_all_rows(fn):
        def group(g, carry):
            for u in range(DMA_UNROLL):
                for k in range(TOP_K):
                    fn(g * DMA_UNROLL + u, k)
            return carry

        lax.fori_loop(0, tm // DMA_UNROLL, group, 0)

    @pl.when(i == 0)
    def _():
        for_all_rows(lambda r, k: row_copy(dest_ref, r, k, 0).start())

    @pl.when(i + 1 < n)
    def _():
        for_all_rows(lambda r, k: row_copy(dest_next_ref, r, k, (i + 1) % 2).start())

    slot = i % 2
    for_all_rows(lambda r, k: row_copy(dest_ref, r, k, slot).wait())
    y = jnp.zeros((tm, D_MODEL), F32)
    for k in range(TOP_K):
        y = y + gate_ref[:, k:k + 1] * buf[slot, k]
    o_ref[...] = _ln(DEEPNORM_ALPHA * x_ref[...] + g2_ref[0] * y) * lng_ref[...] + lnb_ref[...]


def _combine(out_rows, dest, gate_w, x1, mod3, ln_g3, ln_b3, l):
    tm = COMBINE_TM
    nt = T_ALL // tm
    dest2 = dest.reshape(nt, 1, tm * TOP_K)
    row = functools.partial(_mod_row, tm=tm)
    tile = pl.BlockSpec((tm, D_MODEL), lambda i: (i, 0))
    vec = pl.BlockSpec((None, 1, D_MODEL), lambda i: (l, 0, 0))
    return pl.pallas_call(
        _combine_kernel,
        grid=(nt,),
        in_specs=[
            pl.BlockSpec((None, 1, tm * TOP_K), lambda i: (i, 0, 0), memory_space=pltpu.SMEM),
            pl.BlockSpec((None, 1, tm * TOP_K), lambda i: (jnp.minimum(i + 1, nt - 1), 0, 0),
                         memory_space=pltpu.SMEM),
            _ANY,
            pl.BlockSpec((tm, TOP_K), lambda i: (i, 0)),
            tile,
            pl.BlockSpec((1, 1, D_MODEL), lambda i: (row(i), 0, 5)),
            vec, vec,
        ],
        out_specs=tile,
        out_shape=jax.ShapeDtypeStruct((T_ALL, D_MODEL), F32),
        scratch_shapes=[pltpu.VMEM((2, TOP_K, tm, D_MODEL), F32), pltpu.SemaphoreType.DMA((2,))],
        compiler_params=_params(("arbitrary",), 40),
        name="moe_combine",
    )(dest2, dest2, out_rows, gate_w, x1, mod3, ln_g3, ln_b3)


def _rope_tables():
    rows = DEC_SEQ // GRID_W
    r, col = jnp.meshgrid(jnp.arange(rows, dtype=F32), jnp.arange(GRID_W, dtype=F32), indexing="ij")
    freqs = ROPE_BASE ** (-jnp.arange(ROT_HALF, dtype=F32) / ROT_HALF)
    ar = r.reshape(-1)[:, None] * freqs
    ac = col.reshape(-1)[:, None] * freqs
    ang = jnp.concatenate([ar, ar, ac, ac] * 2, axis=-1)
    return jnp.cos(ang), jnp.sin(ang)


def kernel(x_prompt, x_sample, c, cache_k, cache_v, state_hgrn, c_ctx, w_mod, b_mod, w_in, conv_w, conv_b, conv_norm_g, conv_norm_b, w_conv_out, lambda_q1, lambda_k1, lambda_q2, lambda_k2, subln_w, w_attn_out, lb_param, hgrn_norm_w, w_hgrn_out, w_o, ln1_g, ln1_b, ln2_g, ln2_b, w_router, b_router, w1, b1, w2, b2):
    lb_all = jnp.cumsum(jax.nn.softmax(lb_param.astype(F32), axis=0), axis=0)
    lb_all = lb_all - lb_all[:1]
    cos, sin = _rope_tables()
    cond = jnp.concatenate([c_ctx[None, :], c, jnp.zeros((N_MOD_ROWS - 1 - DEC_BATCH, D_MODEL), F32)], axis=0)
    row3 = lambda a: a.reshape(DEPTH, 1, a.shape[-1])
    b_mod3, conv_b3, conv_g3, conv_beta3 = row3(b_mod), row3(conv_b), row3(conv_norm_g), row3(conv_norm_b)
    subln3, nw3, b_router3 = row3(subln_w), row3(hgrn_norm_w), row3(b_router)
    ln1_g3, ln1_b3, ln2_g3, ln2_b3 = row3(ln1_g), row3(ln1_b), row3(ln2_g), row3(ln2_b)
    b1_4 = b1.reshape(DEPTH, N_EXPERTS, 1, 2 * D_FF)
    b2_4 = b2.reshape(DEPTH, N_EXPERTS, 1, D_MODEL)
    w_in, w_conv_out, w_attn_out, w_hgrn_out, w_o = (
        w.astype(BF16) for w in (w_in, w_conv_out, w_attn_out, w_hgrn_out, w_o))

    x = jnp.concatenate([x_prompt.reshape(T_CTX, D_MODEL), x_sample.reshape(T_LAT, D_MODEL)], axis=0)
    ks_new, vs_new, ss_new = [], [], []
    for l in range(DEPTH):
        lam_init = 0.8 - 0.6 * math.exp(-0.3 * l)
        lam = (jnp.exp(jnp.sum(lambda_q1[l].astype(F32) * lambda_k1[l].astype(F32)))
               - jnp.exp(jnp.sum(lambda_q2[l].astype(F32) * lambda_k2[l].astype(F32))) + lam_init).reshape(1, 1)

        mod3 = _modulation(cond, w_mod, b_mod3, l).reshape(N_MOD_ROWS, 1, 6 * D_MODEL)
        proj = _in_proj(x, mod3, w_in, l)

        conv_args = (proj, conv_w, conv_b3, conv_g3, conv_beta3, l)
        ca = (_conv_branch(*conv_args, BATCH, SEQ, 0), _conv_branch(*conv_args, DEC_BATCH, DEC_SEQ, T_CTX))
        ob_ctx, k_new, v_new = _attn_branch(proj, lam, subln3, l, BATCH, SEQ, 0)
        ob = (ob_ctx, _attn_branch(proj, lam, subln3, l, DEC_BATCH, DEC_SEQ, T_CTX,
                                   ctx=(cache_k, cache_v, cos, sin)))
        oc_ctx, s_new = _hgrn_branch(proj, lb_all, nw3, l, BATCH, SEQ, 0)
        oc = (oc_ctx, _hgrn_branch(proj, lb_all, nw3, l, DEC_BATCH, DEC_SEQ, T_CTX, state=state_hgrn))
        ks_new.append(k_new)
        vs_new.append(v_new)
        ss_new.append(s_new)

        mixed = _mix(ca, ob, oc, proj, w_conv_out, w_attn_out, w_hgrn_out, w_o, l)
        x1, u2, logits = _post_mix(x, mixed, mod3, ln1_g3, ln1_b3, w_router, b_router3, l)

        gate_w, dest, row_tok, blk_expert, blk_valid = _route(logits)
        x_rows = _gather_rows(u2, row_tok, blk_valid)
        out_rows = _moe(x_rows, blk_expert, blk_valid, w1, b1_4, w2, b2_4, l)
        x = _combine(out_rows, dest, gate_w, x1, mod3, ln2_g3, ln2_b3, l)

    y_prompt = x[:T_CTX].reshape(BATCH, SEQ, D_MODEL)
    y_sample = x[T_CTX:].reshape(DEC_BATCH, DEC_SEQ, D_MODEL)
    return (y_prompt, y_sample, jnp.stack(ks_new, axis=1), jnp.stack(vs_new, axis=1),
            jnp.stack(ss_new, axis=1))
```

```python
import functools
import math

import jax
import jax.numpy as jnp
from jax import lax
from jax.experimental import pallas as pl
from jax.experimental.pallas import tpu as pltpu

F32 = jnp.float32
BF16 = jnp.bfloat16

D_MODEL = 2048
BATCH = 32
SEQ = 256
DEPTH = 2
DEC_BATCH = 4
DEC_SEQ = 1024
PAST_LEN = 512
GRID_W = 64
D_CONV = D_MODEL // 4
CONV_WIDTH = 31
N_DIFF_HEADS = 8
D_QK = 64
D_V = 2 * D_QK
ROPE_BASE = 10000.0
ROT_HALF = D_QK // 4
HG_HEADS = 4
HG_F = 128
HG_I = (D_MODEL // 4) // HG_HEADS
N_EXPERTS = 32
TOP_K = 4
D_FF = D_MODEL
SWIGLU_ALPHA = 1.702
SWIGLU_LIMIT = 7.0
DEEPNORM_ALPHA = (2 * DEPTH) ** 0.25
LN_EPS = 1e-5

T_CTX = BATCH * SEQ
T_LAT = DEC_BATCH * DEC_SEQ
T_ALL = T_CTX + T_LAT
N_ASSIGN = T_ALL * TOP_K
N_MOD_ROWS = 8

COL_Q = (2 * D_CONV) // 128
COL_K = COL_Q + N_DIFF_HEADS
COL_V = COL_K + N_DIFF_HEADS
COL_HQ = COL_V + N_DIFF_HEADS
COL_HF_FWD = COL_HQ + HG_HEADS
COL_HF_BWD = COL_HF_FWD + HG_HEADS
COL_HI = COL_HF_BWD + HG_HEADS
COL_HG = COL_HI + HG_HEADS
COL_GATES = COL_HG + HG_HEADS
D_IN = COL_GATES * 128 + 3 * D_MODEL

HG_CHUNK = 16
HG_UNROLL = 16
MIX_TM = 512
MIX_TN = 512
HG_GROUP = 128
ATTN_TQ = 256
ATTN_CTX_HEADS = 8
CONV_ROWS = 64
CONV_PAD = 16
MOE_TM = 1024
MOE_SUB = 256
MOE_TN = 512
MOE_BLOCKS = N_ASSIGN // MOE_TM + N_EXPERTS
MOE_ROWS = MOE_BLOCKS * MOE_TM
DMA_UNROLL = 8
COMBINE_TM = 128

MIB = 1024 * 1024


def _params(sem, vmem_mib):
    return pltpu.CompilerParams(dimension_semantics=sem, vmem_limit_bytes=vmem_mib * MIB)


def _mod_row(i, tm):
    r0 = i * tm
    return jnp.where(r0 < T_CTX, 0, 1 + (r0 - T_CTX) // DEC_SEQ)


def _ln(x):
    mu = jnp.mean(x, axis=-1, keepdims=True)
    xc = x - mu
    var = jnp.mean(xc * xc, axis=-1, keepdims=True)
    return xc * lax.rsqrt(var + LN_EPS)


def _bdot(a, b):
    return jnp.dot(a.astype(BF16), b.astype(BF16), preferred_element_type=F32)


_ANY = pl.BlockSpec(memory_space=pl.ANY)


def _mod_kernel(c_ref, w_ref, b_ref, o_ref):
    c = c_ref[...]
    o_ref[...] = _bdot(c * jax.nn.sigmoid(c), w_ref[...]) + b_ref[...]


def _modulation(cond, w_mod, b_mod3, l):
    tn = 1024
    return pl.pallas_call(
        _mod_kernel,
        grid=(6 * D_MODEL // tn,),
        in_specs=[
            pl.BlockSpec((N_MOD_ROWS, D_MODEL), lambda j: (0, 0)),
            pl.BlockSpec((None, D_MODEL, tn), lambda j: (l, 0, j)),
            pl.BlockSpec((None, 1, tn), lambda j: (l, 0, j)),
        ],
        out_specs=pl.BlockSpec((N_MOD_ROWS, tn), lambda j: (0, j)),
        out_shape=jax.ShapeDtypeStruct((N_MOD_ROWS, 6 * D_MODEL), F32),
        compiler_params=_params(("arbitrary",), 40),
        name="modulation",
    )(cond, w_mod, b_mod3)


def _in_proj_kernel(x_ref, sh_ref, sc_ref, w_ref, o_ref, u_ref):
    @pl.when(pl.program_id(1) == 0)
    def _():
        u_ref[...] = (_ln(x_ref[...]) * (1.0 + sc_ref[0]) + sh_ref[0]).astype(BF16)

    o_ref[...] = jnp.dot(u_ref[...], w_ref[...].astype(BF16), preferred_element_type=F32)


def _in_proj(x, mod3, w_in, l):
    tm, tn = 1024, 512
    row = functools.partial(_mod_row, tm=tm)
    return pl.pallas_call(
        _in_proj_kernel,
        grid=(T_ALL // tm, D_IN // tn),
        in_specs=[
            pl.BlockSpec((tm, D_MODEL), lambda i, j: (i, 0)),
            pl.BlockSpec((1, 1, D_MODEL), lambda i, j: (row(i), 0, 0)),
            pl.BlockSpec((1, 1, D_MODEL), lambda i, j: (row(i), 0, 1)),
            pl.BlockSpec((None, D_MODEL, tn), lambda i, j: (l, 0, j)),
        ],
        out_specs=pl.BlockSpec((tm, tn), lambda i, j: (i, j)),
        out_shape=jax.ShapeDtypeStruct((T_ALL, D_IN), F32),
        scratch_shapes=[pltpu.VMEM((tm, D_MODEL), BF16)],
        compiler_params=_params(("parallel", "arbitrary"), 48),
        name="in_proj",
    )(x, mod3, mod3, w_in)


def _conv_kernel(glu_ref, w_ref, b_ref, g_ref, beta_ref, o_ref, pad_ref, *, n):
    a = glu_ref[:, :D_CONV]
    g = glu_ref[:, D_CONV:]
    zeros = jnp.zeros((CONV_PAD, D_CONV), F32)
    pad_ref[0:CONV_PAD, :] = zeros
    pad_ref[CONV_PAD + n:CONV_PAD + n + CONV_PAD, :] = zeros
    pad_ref[CONV_PAD:CONV_PAD + n, :] = a * jax.nn.sigmoid(g)
    first = CONV_PAD - CONV_WIDTH // 2
    for r0 in range(0, n, CONV_ROWS):
        acc = jnp.zeros((CONV_ROWS, D_CONV), F32)
        for j in range(CONV_WIDTH):
            s = r0 + first + j
            acc = acc + w_ref[j:j + 1, :] * pad_ref[s:s + CONV_ROWS, :]
        y = _ln(acc + b_ref[...]) * g_ref[...] + beta_ref[...]
        o_ref[r0:r0 + CONV_ROWS, :] = (y * jax.nn.sigmoid(y)).astype(BF16)


def _conv_branch(proj, conv_w, conv_b3, conv_g3, conv_beta3, l, nb, n, row_off):
    off = row_off // n
    vec = pl.BlockSpec((None, 1, D_CONV), lambda b: (l, 0, 0))
    return pl.pallas_call(
        functools.partial(_conv_kernel, n=n),
        grid=(nb,),
        in_specs=[
            pl.BlockSpec((n, 2 * D_CONV), lambda b: (off + b, 0)),
            pl.BlockSpec((None, CONV_WIDTH, D_CONV), lambda b: (l, 0, 0)),
            vec, vec, vec,
        ],
        out_specs=pl.BlockSpec((n, D_CONV), lambda b: (b, 0)),
        out_shape=jax.ShapeDtypeStruct((nb * n, D_CONV), BF16),
        scratch_shapes=[pltpu.VMEM((n + 2 * CONV_PAD, D_CONV), F32)],
        compiler_params=_params(("parallel",), 40),
        name="conv_branch",
    )(proj, conv_w, conv_b3, conv_g3, conv_beta3)


def _rope(x, cos, sin):
    lane = lax.broadcasted_iota(jnp.int32, x.shape, 1)
    nxt = pltpu.roll(x, 128 - ROT_HALF, 1)
    prv = pltpu.roll(x, ROT_HALF, 1)
    rot = jnp.where((lane % (2 * ROT_HALF)) < ROT_HALF, -nxt, prv)
    return x * cos + rot * sin


def _attn_kernel(*refs, n, latent, lam_init, heads):
    if latent:
        lam_ref, q_ref, k_ref, v_ref, sw_ref, ck_ref, cv_ref, cos_ref, sin_ref, o_ref = refs
    else:
        lam_ref, q_ref, k_ref, v_ref, sw_ref, o_ref, ko_ref, vo_ref = refs
    lam = lam_ref[0, 0]
    scale = D_QK ** -0.5
    nt = (((1,), (1,)), ((), ()))
    first_sub = lax.broadcasted_iota(jnp.int32, (1, 2 * D_QK), 1) < D_QK

    for hh in range(heads):
        cols = slice(hh * 128, (hh + 1) * 128)
        k = k_ref[:, cols]
        v = v_ref[:, cols]
        if latent:
            k = _rope(k, cos_ref[...], sin_ref[...])
            ckb = ck_ref[...].astype(BF16)
            cvb = cv_ref[...].astype(BF16)
        else:
            ko_ref[hh] = k
            vo_ref[hh] = v
        kb = k.astype(BF16)
        vb = v.astype(BF16)

        def softmax_maps(qx):
            s = lax.dot_general(qx, kb, nt, preferred_element_type=F32) * scale
            m = jnp.max(s, axis=-1, keepdims=True)
            if latent:
                sc = lax.dot_general(qx, ckb, nt, preferred_element_type=F32) * scale
                m = jnp.maximum(m, jnp.max(sc, axis=-1, keepdims=True))
                e = jnp.exp(s - m)
                ec = jnp.exp(sc - m)
                den = jnp.sum(e, axis=-1, keepdims=True) + jnp.sum(ec, axis=-1, keepdims=True)
                return e / den, ec / den
            e = jnp.exp(s - m)
            return e / jnp.sum(e, axis=-1, keepdims=True), None

        for r0 in range(0, n, ATTN_TQ):
            q = q_ref[r0:r0 + ATTN_TQ, cols]
            if latent:
                q = _rope(q, cos_ref[r0:r0 + ATTN_TQ, :], sin_ref[r0:r0 + ATTN_TQ, :])
            p1, pc1 = softmax_maps(jnp.where(first_sub, q, 0.0).astype(BF16))
            p2, pc2 = softmax_maps(jnp.where(first_sub, 0.0, q).astype(BF16))
            o = _bdot(p1 - lam * p2, vb)
            if latent:
                o = o + _bdot(pc1 - lam * pc2, cvb)
            o = o * lax.rsqrt(jnp.mean(o * o, axis=-1, keepdims=True) + LN_EPS) * sw_ref[...]
            o_ref[r0:r0 + ATTN_TQ, cols] = (o * (1.0 - lam_init)).astype(BF16)


def _attn_branch(proj, lam, subln3, l, nb, n, row_off, ctx=None):
    off = row_off // n
    latent = ctx is not None
    heads = 1 if latent else ATTN_CTX_HEADS
    assert COL_Q % heads == 0 and N_DIFF_HEADS % heads == 0
    lam_init = 0.8 - 0.6 * math.exp(-0.3 * l)
    tok = lambda col: pl.BlockSpec((n, heads * 128), lambda b, h: (off + b, col // heads + h))
    in_specs = [pl.BlockSpec(memory_space=pltpu.SMEM), tok(COL_Q), tok(COL_K), tok(COL_V),
                pl.BlockSpec((None, 1, D_V), lambda b, h: (l, 0, 0))]
    args = [lam, proj, proj, proj, subln3]
    o_spec = pl.BlockSpec((n, heads * D_V), lambda b, h: (b, h))
    o_shape = jax.ShapeDtypeStruct((nb * n, N_DIFF_HEADS * D_V), BF16)
    if latent:
        cache_k, cache_v, cos, sin = ctx
        cache = pl.BlockSpec((None, None, None, PAST_LEN, 128), lambda b, h: (b, l, h, 0, 0))
        table = pl.BlockSpec((n, 128), lambda b, h: (0, 0))
        in_specs += [cache, cache, table, table]
        args += [cache_k, cache_v, cos, sin]
        out_specs, out_shape = o_spec, o_shape
    else:
        kv_spec = pl.BlockSpec((None, heads, n, 128), lambda b, h: (b, h, 0, 0))
        kv_shape = jax.ShapeDtypeStruct((nb, N_DIFF_HEADS, n, 128), F32)
        out_specs, out_shape = [o_spec, kv_spec, kv_spec], [o_shape, kv_shape, kv_shape]
    return pl.pallas_call(
        functools.partial(_attn_kernel, n=n, latent=latent, lam_init=lam_init, heads=heads),
        grid=(nb, N_DIFF_HEADS // heads),
        in_specs=in_specs,
        out_specs=out_specs,
        out_shape=out_shape,
        compiler_params=_params(("parallel", "parallel"), 40),
        name="attn_latent" if latent else "attn_context",
    )(*args)


def _hgrn_kernel(*refs, n, has_state):
    if has_state:
        (hq_ref, ff_ref, fb_ref, hi_ref, hg_ref, lb_ref, nw_ref, s0_ref, o_ref,
         qd_scr, kd_scr, ee_scr, o_scr, st_scr) = refs
    else:
        (hq_ref, ff_ref, fb_ref, hi_ref, hg_ref, lb_ref, nw_ref, o_ref, so_ref,
         qd_scr, kd_scr, ee_scr, o_scr, st_scr) = refs
    C, G = HG_CHUNK, HG_GROUP
    nc = n // C
    hq = hq_ref[...]
    q = hq * jax.nn.sigmoid(hq)
    q3 = q.reshape(nc, C, HG_F)
    vb = hi_ref[...].astype(BF16)
    srow = lax.broadcasted_iota(jnp.int32, (n, 1), 0)
    s_loc = srow % C
    lane = lax.broadcasted_iota(jnp.int32, (n, G), 1)
    lane_base = ((srow % G) // C) * C
    gr = lax.broadcasted_iota(jnp.int32, (G, G), 0)
    gc = lax.broadcasted_iota(jnp.int32, (G, G), 1)
    same_chunk = (gr // C) == (gc // C)
    tn = (((0,), (0,)), ((), ()))
    nt = (((1,), (1,)), ((), ()))
    chunk_rows = lambda a3: jnp.broadcast_to(a3, (nc, C, HG_F)).reshape(n, HG_F)

    for d, f_ref in enumerate((ff_ref, fb_ref)):
        fwd = d == 0
        lb = lb_ref[d:d + 1, :]
        f = lb + (1.0 - lb) * jax.nn.sigmoid(f_ref[...])
        lf = jnp.log(f)
        k = 1.0 - f
        tri = (same_chunk & ((gc <= gr) if fwd else (gc >= gr))).astype(F32)
        cum = jnp.concatenate(
            [jnp.dot(tri, lf[g:g + G, :], precision=lax.Precision.HIGHEST, preferred_element_type=F32)
             for g in range(0, n, G)], axis=0)
        cum3 = cum.reshape(nc, C, HG_F)
        edge = chunk_rows(cum3[:, C - 1:C, :] if fwd else cum3[:, 0:1, :])
        qd_scr[d] = (q * jnp.exp(cum)).astype(BF16)
        kd_scr[d] = (k * jnp.exp(edge - cum)).astype(BF16)
        ee_scr[d] = jnp.exp(edge)
        scores_t = jnp.zeros((n, G), F32)
        for t in range(C):
            seen = (s_loc <= t) if fwd else (s_loc >= t)
            rel = jnp.where(seen, chunk_rows(cum3[:, t:t + 1, :]) - cum, -jnp.inf)
            part = jnp.sum(jnp.exp(rel) * k * chunk_rows(q3[:, t:t + 1, :]), axis=-1, keepdims=True)
            scores_t = jnp.where(lane == lane_base + t, part, scores_t)
        sb = scores_t.astype(BF16)
        for g in range(0, n, G):
            o_scr[d, g:g + G, :] = lax.dot_general(sb[g:g + G, :], vb[g:g + G, :], tn,
                                                   preferred_element_type=F32)
        st_scr[d] = s0_ref[d].T if has_state else jnp.zeros((HG_I, HG_F), F32)

    def step(d, r0):
        st = st_scr[d]
        o_scr[d, pl.ds(r0, C), :] += lax.dot_general(
            qd_scr[d, pl.ds(r0, C), :], st.astype(BF16), nt, preferred_element_type=F32)
        upd = lax.dot_general(hi_ref[pl.ds(r0, C), :].astype(BF16), kd_scr[d, pl.ds(r0, C), :], tn,
                              preferred_element_type=F32)
        st_scr[d] = ee_scr[d, pl.ds(r0, C), :][0:1, :] * st + upd

    def body(c, carry):
        step(0, pl.multiple_of(c * C, C))
        step(1, pl.multiple_of((nc - 1 - c) * C, C))
        return carry

    lax.fori_loop(0, nc, body, 0, unroll=HG_UNROLL)

    o = o_scr[0] + o_scr[1]
    o = o * lax.rsqrt(jnp.mean(o * o, axis=-1, keepdims=True) + LN_EPS) * nw_ref[...]
    hg = hg_ref[...]
    o_ref[...] = (o * (hg * jax.nn.sigmoid(hg))).astype(BF16)
    if not has_state:
        so_ref[0] = st_scr[0].T
        so_ref[1] = st_scr[1].T


def _hgrn_branch(proj, lb_all, nw3, l, nb, n, row_off, state=None):
    off = row_off // n
    has_state = state is not None
    tok = lambda col: pl.BlockSpec((n, 128), lambda b, h: (off + b, col + h))
    in_specs = [tok(COL_HQ), tok(COL_HF_FWD), tok(COL_HF_BWD), tok(COL_HI), tok(COL_HG),
                pl.BlockSpec((None, 2, HG_F), lambda b, h: (l, 0, h)),
                pl.BlockSpec((None, 1, HG_I), lambda b, h: (l, 0, 0))]
    args = [proj] * 5 + [lb_all, nw3]
    o_spec = pl.BlockSpec((n, HG_I), lambda b, h: (b, h))
    o_shape = jax.ShapeDtypeStruct((nb * n, HG_HEADS * HG_I), BF16)
    if has_state:
        in_specs.append(pl.BlockSpec((None, None, 2, None, HG_F, HG_I), lambda b, h: (b, l, 0, h, 0, 0)))
        args.append(state)
        out_specs, out_shape = o_spec, o_shape
    else:
        out_specs = [o_spec, pl.BlockSpec((None, 2, None, HG_F, HG_I), lambda b, h: (b, 0, h, 0, 0))]
        out_shape = [o_shape, jax.ShapeDtypeStruct((nb, 2, HG_HEADS, HG_F, HG_I), F32)]
    return pl.pallas_call(
        functools.partial(_hgrn_kernel, n=n, has_state=has_state),
        grid=(nb, HG_HEADS),
        in_specs=in_specs,
        out_specs=out_specs,
        out_shape=out_shape,
        scratch_shapes=[pltpu.VMEM((2, n, HG_F), BF16), pltpu.VMEM((2, n, HG_F), BF16),
                        pltpu.VMEM((2, n, HG_F), F32), pltpu.VMEM((2, n, HG_I), F32),
                        pltpu.VMEM((2, HG_I, HG_F), F32)],
        compiler_params=_params(("parallel", "parallel"), 40),
        name="hgrn_latent" if has_state else "hgrn_context",
    )(*args)


def _mix_kernel(ca_c_ref, ca_l_ref, ob_c_ref, ob_l_ref, oc_c_ref, oc_l_ref, ga_ref, gb_ref, gc_ref,
                wc_ref, wa_ref, wh_ref, wo_ref, o_ref, *, ctx_tiles):
    j = pl.program_id(1)
    is_ctx = pl.program_id(0) < ctx_tiles
    pick = lambda c_ref, l_ref: jnp.where(is_ctx, c_ref[...], l_ref[...])
    merged = (jax.nn.sigmoid(ga_ref[...]) * _bdot(pick(ca_c_ref, ca_l_ref), wc_ref[...])
              + jax.nn.sigmoid(gb_ref[...]) * _bdot(pick(ob_c_ref, ob_l_ref), wa_ref[...])
              + jax.nn.sigmoid(gc_ref[...]) * _bdot(pick(oc_c_ref, oc_l_ref), wh_ref[...]))
    part = _bdot(merged, wo_ref[...])

    @pl.when(j == 0)
    def _():
        o_ref[...] = part

    @pl.when(j > 0)
    def _():
        o_ref[...] += part


def _mix(ca, ob, oc, proj, w_conv_out, w_attn_out, w_hgrn_out, w_o, l):
    tm, tn = MIX_TM, MIX_TN
    assert (COL_GATES * 128) % tn == 0 and D_MODEL % tn == 0 and T_CTX % tm == 0
    nj = D_MODEL // tn
    gate0 = COL_GATES * 128 // tn
    ctx_tiles = T_CTX // tm
    pair = lambda w: [pl.BlockSpec((tm, w), lambda i, j: (jnp.minimum(i, ctx_tiles - 1), 0)),
                      pl.BlockSpec((tm, w), lambda i, j: (jnp.maximum(i - ctx_tiles, 0), 0))]
    return pl.pallas_call(
        functools.partial(_mix_kernel, ctx_tiles=ctx_tiles),
        grid=(T_ALL // tm, nj),
        in_specs=pair(D_CONV) + pair(N_DIFF_HEADS * D_V) + pair(HG_HEADS * HG_I) + [
            pl.BlockSpec((tm, tn), lambda i, j: (i, gate0 + j)),
            pl.BlockSpec((tm, tn), lambda i, j: (i, gate0 + nj + j)),
            pl.BlockSpec((tm, tn), lambda i, j: (i, gate0 + 2 * nj + j)),
            pl.BlockSpec((None, D_CONV, tn), lambda i, j: (l, 0, j)),
            pl.BlockSpec((None, N_DIFF_HEADS * D_V, tn), lambda i, j: (l, 0, j)),
            pl.BlockSpec((None, HG_HEADS * HG_I, tn), lambda i, j: (l, 0, j)),
            pl.BlockSpec((None, tn, D_MODEL), lambda i, j: (l, j, 0)),
        ],
        out_specs=pl.BlockSpec((tm, D_MODEL), lambda i, j: (i, 0)),
        out_shape=jax.ShapeDtypeStruct((T_ALL, D_MODEL), F32),
        compiler_params=_params(("parallel", "arbitrary"), 48),
        name="mix",
    )(*ca, *ob, *oc, proj, proj, proj, w_conv_out, w_attn_out, w_hgrn_out, w_o)


def _post_mix_kernel(x_ref, m_ref, g1_ref, sh2_ref, sc2_ref, lng_ref, lnb_ref, wr_ref, br_ref,
                     x1_ref, u2_ref, lg_ref):
    x1 = _ln(DEEPNORM_ALPHA * x_ref[...] + g1_ref[0] * m_ref[...]) * lng_ref[...] + lnb_ref[...]
    x1_ref[...] = x1
    u2 = _ln(x1) * (1.0 + sc2_ref[0]) + sh2_ref[0]
    u2_ref[...] = u2
    lg_ref[...] = jnp.dot(u2, wr_ref[...], precision=lax.Precision.HIGHEST,
                          preferred_element_type=F32) + br_ref[...]


def _post_mix(x, mixed, mod3, ln_g3, ln_b3, w_router, b_router3, l):
    tm = 256
    row = functools.partial(_mod_row, tm=tm)
    tile = pl.BlockSpec((tm, D_MODEL), lambda i: (i, 0))
    mod_spec = lambda k: pl.BlockSpec((1, 1, D_MODEL), lambda i: (row(i), 0, k))
    vec = pl.BlockSpec((None, 1, D_MODEL), lambda i: (l, 0, 0))
    return pl.pallas_call(
        _post_mix_kernel,
        grid=(T_ALL // tm,),
        in_specs=[
            tile, tile, mod_spec(2), mod_spec(3), mod_spec(4), vec, vec,
            pl.BlockSpec((None, D_MODEL, N_EXPERTS), lambda i: (l, 0, 0)),
            pl.BlockSpec((None, 1, N_EXPERTS), lambda i: (l, 0, 0)),
        ],
        out_specs=[tile, tile, pl.BlockSpec((tm, N_EXPERTS), lambda i: (i, 0))],
        out_shape=[
            jax.ShapeDtypeStruct((T_ALL, D_MODEL), F32),
            jax.ShapeDtypeStruct((T_ALL, D_MODEL), F32),
            jax.ShapeDtypeStruct((T_ALL, N_EXPERTS), F32),
        ],
        compiler_params=_params(("parallel",), 40),
        name="post_mix",
    )(x, mixed, mod3, mod3, mod3, ln_g3, ln_b3, w_router, b_router3)


def _route(logits):
    top_vals, top_idx = lax.top_k(logits, TOP_K)
    gate_w = jax.nn.softmax(top_vals, axis=-1)
    e_flat = top_idx.reshape(-1).astype(jnp.int32)
    onehot = (e_flat[:, None] == jnp.arange(N_EXPERTS, dtype=jnp.int32)[None, :]).astype(jnp.int32)
    seen = jnp.cumsum(onehot, axis=0)
    counts = seen[-1]
    padded = ((counts + MOE_TM - 1) // MOE_TM) * MOE_TM
    p_end = jnp.cumsum(padded)
    p_start = p_end - padded
    dest = jnp.sum(onehot * (p_start[None, :] + seen - 1), axis=1)
    row_tok = jnp.zeros((MOE_ROWS,), jnp.int32).at[dest].set(
        jnp.arange(N_ASSIGN, dtype=jnp.int32) // TOP_K)
    blk_start = jnp.arange(MOE_BLOCKS, dtype=jnp.int32) * MOE_TM
    blk_expert = jnp.minimum(jnp.sum((p_end[None, :] <= blk_start[:, None]).astype(jnp.int32), axis=1),
                             N_EXPERTS - 1)
    blk_valid = jnp.clip(counts[blk_expert] - (blk_start - p_start[blk_expert]), 0, MOE_TM)
    return gate_w, dest, row_tok, blk_expert.astype(jnp.int32), blk_valid.astype(jnp.int32)


def _gather_kernel(nv_ref, tok_ref, tok_next_ref, u_hbm, o_ref, buf, sem):
    b = pl.program_id(0)
    nb = pl.num_programs(0)

    n_sub = MOE_TM // MOE_SUB

    def row_copy(tok, s, q, slot):
        r = s * MOE_SUB + q
        return pltpu.make_async_copy(u_hbm.at[pl.ds(tok[0, r], 1)], buf.at[slot, pl.ds(r, 1)], sem.at[slot, s])

    def for_sub_block_rows(tok, s, slot, fn):
        def group(g, carry):
            for u in range(DMA_UNROLL):
                fn(row_copy(tok, s, g * DMA_UNROLL + u, slot))
            return carry

        lax.fori_loop(0, MOE_SUB // DMA_UNROLL, group, 0)

    def issue(tok, blk, slot):
        for s in range(n_sub):
            @pl.when(s * MOE_SUB < nv_ref[blk])
            def _():
                for_sub_block_rows(tok, s, slot, lambda copy: copy.start())

    @pl.when(b == 0)
    def _():
        issue(tok_ref, 0, 0)

    for slot in range(2):
        @pl.when(jnp.logical_and(b + 1 < nb, (b + 1) % 2 == slot))
        def _():
            issue(tok_next_ref, b + 1, slot)

    for slot in range(2):
        @pl.when(b % 2 == slot)
        def _():
            for s in range(n_sub):
                rows = pl.ds(s * MOE_SUB, MOE_SUB)

                @pl.when(s * MOE_SUB < nv_ref[b])
                def _():
                    for_sub_block_rows(tok_ref, s, slot, lambda copy: copy.wait())
                    o_ref[rows, :] = buf[slot, rows, :].astype(BF16)

                @pl.when(s * MOE_SUB >= nv_ref[b])
                def _():
                    o_ref[rows, :] = jnp.zeros((MOE_SUB, D_MODEL), BF16)


def _gather_rows(u2, row_tok, blk_valid):
    tok2 = row_tok.reshape(MOE_BLOCKS, 1, MOE_TM)
    grid_spec = pltpu.PrefetchScalarGridSpec(
        num_scalar_prefetch=1,
        grid=(MOE_BLOCKS,),
        in_specs=[
            pl.BlockSpec((None, 1, MOE_TM), lambda b, nv: (b, 0, 0), memory_space=pltpu.SMEM),
            pl.BlockSpec((None, 1, MOE_TM), lambda b, nv: (jnp.minimum(b + 1, MOE_BLOCKS - 1), 0, 0),
                         memory_space=pltpu.SMEM),
            _ANY,
        ],
        out_specs=pl.BlockSpec((MOE_TM, D_MODEL), lambda b, nv: (b, 0)),
        scratch_shapes=[pltpu.VMEM((2, MOE_TM, D_MODEL), F32),
                        pltpu.SemaphoreType.DMA((2, MOE_TM // MOE_SUB))],
    )
    return pl.pallas_call(
        _gather_kernel,
        grid_spec=grid_spec,
        out_shape=jax.ShapeDtypeStruct((MOE_ROWS, D_MODEL), BF16),
        compiler_params=_params(("arbitrary",), 40),
        name="moe_gather",
    )(blk_valid, tok2, tok2, u2)


def _moe_kernel(be_ref, nv_ref, x_ref, w1_ref, b1_ref, w2_ref, b2_ref, o_ref, w1b_ref, w2b_ref):
    blk = pl.program_id(0)
    j = pl.program_id(1)
    nvalid = nv_ref[blk]

    @pl.when(j == 0)
    def _():
        o_ref[...] = jnp.zeros_like(o_ref)

    r = lax.broadcasted_iota(jnp.int32, (MOE_TN, MOE_TN // 2), 0)
    c = lax.broadcasted_iota(jnp.int32, (MOE_TN, MOE_TN // 2), 1)

    def sub_block(r0, rows, w1b, w2b):
        h = jnp.dot(x_ref[pl.ds(r0, rows), :], w1b, preferred_element_type=F32) + b1_ref[...]
        glu = jnp.minimum(h, SWIGLU_LIMIT)
        lin = jnp.clip(pltpu.roll(h, MOE_TN - 1, 1), -SWIGLU_LIMIT, SWIGLU_LIMIT)
        act = glu * jax.nn.sigmoid(SWIGLU_ALPHA * glu) * (lin + 1.0)
        pick_even = (r == 2 * c).astype(BF16)
        act = jnp.dot(act.astype(BF16), pick_even, preferred_element_type=F32)
        o_ref[pl.ds(r0, rows), :] += jnp.dot(act.astype(BF16), w2b, preferred_element_type=F32)

    nsub = (nvalid + MOE_SUB - 1) // MOE_SUB

    @pl.when(nsub == MOE_TM // MOE_SUB)
    def _():
        sub_block(0, MOE_TM, w1_ref[...].astype(BF16), w2_ref[...].astype(BF16))

    @pl.when(jnp.logical_and(nsub > 0, nsub < MOE_TM // MOE_SUB))
    def _():
        w1b_ref[...] = w1_ref[...].astype(BF16)
        w2b_ref[...] = w2_ref[...].astype(BF16)

        def pair(p, carry):
            sub_block(pl.multiple_of(p * (2 * MOE_SUB), 2 * MOE_SUB), 2 * MOE_SUB,
                      w1b_ref[...], w2b_ref[...])
            return carry

        lax.fori_loop(0, nsub // 2, pair, 0)

        @pl.when(nsub % 2 == 1)
        def _():
            sub_block(pl.multiple_of((nsub - 1) * MOE_SUB, MOE_SUB), MOE_SUB, w1b_ref[...], w2b_ref[...])

    @pl.when(j == pl.num_programs(1) - 1)
    def _():
        o_ref[...] = o_ref[...] + b2_ref[...]


def _moe(x_rows, blk_expert, blk_valid, w1, b1_4, w2, b2_4, l):
    nj = 2 * D_FF // MOE_TN
    jj = lambda b, j, nv: jnp.where(nv[b] > 0, j, nj - 1)
    grid_spec = pltpu.PrefetchScalarGridSpec(
        num_scalar_prefetch=2,
        grid=(MOE_BLOCKS, nj),
        in_specs=[
            pl.BlockSpec((MOE_TM, D_MODEL), lambda b, j, be, nv: (b, 0)),
            pl.BlockSpec((None, None, D_MODEL, MOE_TN), lambda b, j, be, nv: (l, be[b], 0, jj(b, j, nv))),
            pl.BlockSpec((None, None, 1, MOE_TN), lambda b, j, be, nv: (l, be[b], 0, jj(b, j, nv))),
            pl.BlockSpec((None, None, MOE_TN // 2, D_MODEL), lambda b, j, be, nv: (l, be[b], jj(b, j, nv), 0)),
            pl.BlockSpec((None, None, 1, D_MODEL), lambda b, j, be, nv: (l, be[b], 0, 0)),
        ],
        out_specs=pl.BlockSpec((MOE_TM, D_MODEL), lambda b, j, be, nv: (b, 0)),
        scratch_shapes=[pltpu.VMEM((D_MODEL, MOE_TN), BF16), pltpu.VMEM((MOE_TN // 2, D_MODEL), BF16)],
    )
    return pl.pallas_call(
        _moe_kernel,
        grid_spec=grid_spec,
        out_shape=jax.ShapeDtypeStruct((MOE_ROWS, D_MODEL), F32),
        compiler_params=_params(("arbitrary", "arbitrary"), 56),
        name="moe",
    )(blk_expert, blk_valid, x_rows, w1, b1_4, w2, b2_4)


def _combine_kernel(dest_ref, dest_next_ref, rows_hbm, gate_ref, x_ref, g2_ref, lng_ref, lnb_ref,
                    o_ref, buf, sem):
    i = pl.program_id(0)
    n = pl.num_programs(0)
    tm = COMBINE_TM

    def row_copy(dest, r, k, slot):
        return pltpu.make_async_copy(rows_hbm.at[pl.ds(dest[0, r * TOP_K + k], 1)],
                                     buf.at[slot, k, pl.ds(r, 1)], sem.at[slot])

    def for_all_rows(fn):
        def group(g, carry):
            for u in range(DMA_UNROLL):
                for k in range(TOP_K):
                    fn(g * DMA_UNROLL + u, k)
            return carry

        lax.fori_loop(0, tm // DMA_UNROLL, group, 0)

    @pl.when(i == 0)
    def _():
        for_all_rows(lambda r, k: row_copy(dest_ref, r, k, 0).start())

    @pl.when(i + 1 < n)
    def _():
        for_all_rows(lambda r, k: row_copy(dest_next_ref, r, k, (i + 1) % 2).start())

    slot = i % 2
    for_all_rows(lambda r, k: row_copy(dest_ref, r, k, slot).wait())
    y = jnp.zeros((tm, D_MODEL), F32)
    for k in range(TOP_K):
        y = y + gate_ref[:, k:k + 1] * buf[slot, k]
    o_ref[...] = _ln(DEEPNORM_ALPHA * x_ref[...] + g2_ref[0] * y) * lng_ref[...] + lnb_ref[...]


def _combine(out_rows, dest, gate_w, x1, mod3, ln_g3, ln_b3, l):
    tm = COMBINE_TM
    nt = T_ALL // tm
    dest2 = dest.reshape(nt, 1, tm * TOP_K)
    row = functools.partial(_mod_row, tm=tm)
    tile = pl.BlockSpec((tm, D_MODEL), lambda i: (i, 0))
    vec = pl.BlockSpec((None, 1, D_MODEL), lambda i: (l, 0, 0))
    return pl.pallas_call(
        _combine_kernel,
        grid=(nt,),
        in_specs=[
            pl.BlockSpec((None, 1, tm * TOP_K), lambda i: (i, 0, 0), memory_space=pltpu.SMEM),
            pl.BlockSpec((None, 1, tm * TOP_K), lambda i: (jnp.minimum(i + 1, nt - 1), 0, 0),
                         memory_space=pltpu.SMEM),
            _ANY,
            pl.BlockSpec((tm, TOP_K), lambda i: (i, 0)),
            tile,
            pl.BlockSpec((1, 1, D_MODEL), lambda i: (row(i), 0, 5)),
            vec, vec,
        ],
        out_specs=tile,
        out_shape=jax.ShapeDtypeStruct((T_ALL, D_MODEL), F32),
        scratch_shapes=[pltpu.VMEM((2, TOP_K, tm, D_MODEL), F32), pltpu.SemaphoreType.DMA((2,))],
        compiler_params=_params(("arbitrary",), 40),
        name="moe_combine",
    )(dest2, dest2, out_rows, gate_w, x1, mod3, ln_g3, ln_b3)


def _rope_tables():
    rows = DEC_SEQ // GRID_W
    r, col = jnp.meshgrid(jnp.arange(rows, dtype=F32), jnp.arange(GRID_W, dtype=F32), indexing="ij")
    freqs = ROPE_BASE ** (-jnp.arange(ROT_HALF, dtype=F32) / ROT_HALF)
    ar = r.reshape(-1)[:, None] * freqs
    ac = col.reshape(-1)[:, None] * freqs
    ang = jnp.concatenate([ar, ar, ac, ac] * 2, axis=-1)
    return jnp.cos(ang), jnp.sin(ang)


def kernel(x_prompt, x_sample, c, cache_k, cache_v, state_hgrn, c_ctx, w_mod, b_mod, w_in, conv_w, conv_b, conv_norm_g, conv_norm_b, w_conv_out, lambda_q1, lambda_k1, lambda_q2, lambda_k2, subln_w, w_attn_out, lb_param, hgrn_norm_w, w_hgrn_out, w_o, ln1_g, ln1_b, ln2_g, ln2_b, w_router, b_router, w1, b1, w2, b2):
    lb_all = jnp.cumsum(jax.nn.softmax(lb_param.astype(F32), axis=0), axis=0)
    lb_all = lb_all - lb_all[:1]
    cos, sin = _rope_tables()
    cond = jnp.concatenate([c_ctx[None, :], c, jnp.zeros((N_MOD_ROWS - 1 - DEC_BATCH, D_MODEL), F32)], axis=0)
    row3 = lambda a: a.reshape(DEPTH, 1, a.shape[-1])
    b_mod3, conv_b3, conv_g3, conv_beta3 = row3(b_mod), row3(conv_b), row3(conv_norm_g), row3(conv_norm_b)
    subln3, nw3, b_router3 = row3(subln_w), row3(hgrn_norm_w), row3(b_router)
    ln1_g3, ln1_b3, ln2_g3, ln2_b3 = row3(ln1_g), row3(ln1_b), row3(ln2_g), row3(ln2_b)
    b1_4 = b1.reshape(DEPTH, N_EXPERTS, 1, 2 * D_FF)
    b2_4 = b2.reshape(DEPTH, N_EXPERTS, 1, D_MODEL)
    w_in, w_conv_out, w_attn_out, w_hgrn_out, w_o = (
        w.astype(BF16) for w in (w_in, w_conv_out, w_attn_out, w_hgrn_out, w_o))

    x = jnp.concatenate([x_prompt.reshape(T_CTX, D_MODEL), x_sample.reshape(T_LAT, D_MODEL)], axis=0)
    ks_new, vs_new, ss_new = [], [], []
    for l in range(DEPTH):
        lam_init = 0.8 - 0.6 * math.exp(-0.3 * l)
        lam = (jnp.exp(jnp.sum(lambda_q1[l].astype(F32) * lambda_k1[l].astype(F32)))
               - jnp.exp(jnp.sum(lambda_q2[l].astype(F32) * lambda_k2[l].astype(F32))) + lam_init).reshape(1, 1)

        mod3 = _modulation(cond, w_mod, b_mod3, l).reshape(N_MOD_ROWS, 1, 6 * D_MODEL)
        proj = _in_proj(x, mod3, w_in, l)

        conv_args = (proj, conv_w, conv_b3, conv_g3, conv_beta3, l)
        ca = (_conv_branch(*conv_args, BATCH, SEQ, 0), _conv_branch(*conv_args, DEC_BATCH, DEC_SEQ, T_CTX))
        ob_ctx, k_new, v_new = _attn_branch(proj, lam, subln3, l, BATCH, SEQ, 0)
        ob = (ob_ctx, _attn_branch(proj, lam, subln3, l, DEC_BATCH, DEC_SEQ, T_CTX,
                                   ctx=(cache_k, cache_v, cos, sin)))
        oc_ctx, s_new = _hgrn_branch(proj, lb_all, nw3, l, BATCH, SEQ, 0)
        oc = (oc_ctx, _hgrn_branch(proj, lb_all, nw3, l, DEC_BATCH, DEC_SEQ, T_CTX, state=state_hgrn))
        ks_new.append(k_new)
        vs_new.append(v_new)
        ss_new.append(s_new)

        mixed = _mix(ca, ob, oc, proj, w_conv_out, w_attn_out, w_hgrn_out, w_o, l)
        x1, u2, logits = _post_mix(x, mixed, mod3, ln1_g3, ln1_b3, w_router, b_router3, l)

        gate_w, dest, row_tok, blk_expert, blk_valid = _route(logits)
        x_rows = _gather_rows(u2, row_tok, blk_valid)
        out_rows = _moe(x_rows, blk_expert, blk_valid, w1, b1_4, w2, b2_4, l)
        x = _combine(out_rows, dest, gate_w, x1, mod3, ln2_g3, ln2_b3, l)

    y_prompt = x[:T_CTX].reshape(BATCH, SEQ, D_MODEL)
    y_sample = x[T_CTX:].reshape(DEC_BATCH, DEC_SEQ, D_MODEL)
    return (y_prompt, y_sample, jnp.stack(ks_new, axis=1), jnp.stack(vs_new, axis=1),
            jnp.stack(ss_new, axis=1))
```
